```python
import jax
import jax.numpy as jnp
from jax import lax
import numpy as np

D_MODEL = 1024
BATCH = 2
SEQ = 8192
DEPTH = 4
DEC_BATCH = 128
DEC_SEQ = 1
PAST_LEN = 2048
PAGE_SIZE = 128

HEAD_DIM = 64
ATT_W = D_MODEL // 2
ATT_HEADS = ATT_W // HEAD_DIM
HG_W = D_MODEL // 4
HG_HEADS = HG_W // HEAD_DIM
LRU_W = D_MODEL // 4
LRU_BLOCKS = 4
LRU_BW = LRU_W // LRU_BLOCKS
CONV_W = 4
LRU_C = 8.0
MIX_W = ATT_W + HG_W + LRU_W
IN_WIDTHS = (ATT_W, ATT_W, ATT_W, HG_W, HG_W, HG_W, HG_W, LRU_W, LRU_W)
IN_COLS = sum(IN_WIDTHS)
Q_BLOCK = 128
SB_BIAS_INIT = -6.0
HG_CHUNK = 64
N_EXPERTS = 32
TOP_K = 4
D_FF = D_MODEL
SWIGLU_LIMIT = 7.0
SWIGLU_ALPHA = 1.702
MOE_BLOCK = 128
LN_EPS = 1e-5
RMS_EPS = 1e-6
DN_ALPHA = (2 * DEPTH) ** 0.25
DN_BETA = (8 * DEPTH) ** -0.25

kernel_name = 'hymba_hgrn2_rglru_stickbreak_moe_step'


def split_cols(a, widths):
    outs = []
    start = 0
    for w in widths:
        outs.append(a[..., start:start + w])
        start += w
    return outs


def layer_norm(x, g, b):
    xf = x.astype(jnp.float32)
    mu = jnp.mean(xf, -1, keepdims=True)
    var = jnp.mean(jnp.square(xf - mu), -1, keepdims=True)
    return ((xf - mu) * lax.rsqrt(var + LN_EPS) * g + b).astype(x.dtype)


def stick_breaking(q, k, v, q_pos, k_pos, bias):
    z = (jnp.einsum('bqhd,bkhd->bhqk', q.astype(jnp.float32), k.astype(jnp.float32)) * HEAD_DIM ** -0.5
         + bias.astype(jnp.float32)[None, :, None, None])
    mask = k_pos[None, :] < q_pos[:, None]
    log_stay = jnp.where(mask, jax.nn.log_sigmoid(-z), 0.0)
    incl = lax.cumsum(log_stay, axis=3, reverse=True)
    after = jnp.concatenate([incl[..., 1:], jnp.zeros_like(incl[..., :1])], axis=-1)
    w = jnp.where(mask, jnp.exp(jax.nn.log_sigmoid(z) + after), 0.0)
    return jnp.einsum('bhqk,bkhd->bqhd', w, v.astype(jnp.float32)).astype(v.dtype)


def hgrn2_mix(q, k, v, log_f, S0):
    B, T, H, _ = q.shape
    C = HG_CHUNK if T % HG_CHUNK == 0 else T
    n = T // C

    def to_chunks(a):
        return a.astype(jnp.float32).reshape(B, n, C, H, a.shape[-1]).transpose(1, 0, 3, 2, 4)

    causal = jnp.tril(jnp.ones((C, C), bool))

    def step(S, inp):
        qc, kc, vc, lfc = inp
        b = jnp.cumsum(lfc, axis=2)
        o_inter = jnp.einsum('bhtd,bhde->bhte', qc * jnp.exp(b), S)
        rel = jnp.exp(jnp.where(causal[None, None, :, :, None],
                                b[:, :, :, None, :] - b[:, :, None, :, :], -jnp.inf))
        att = jnp.einsum('bhtd,bhsd,bhtsd->bhts', qc, kc, rel)
        o = o_inter + jnp.einsum('bhts,bhse->bhte', att, vc)
        b_last = b[:, :, -1:, :]
        S_new = (jnp.exp(b_last[:, :, 0, :])[..., None] * S
                 + jnp.einsum('bhsd,bhse->bhde', kc * jnp.exp(b_last - b), vc))
        return S_new, o

    S_fin, o = lax.scan(step, S0.astype(jnp.float32),
                        (to_chunks(q), to_chunks(k), to_chunks(v), to_chunks(log_f)))
    o = o.transpose(1, 0, 3, 2, 4).reshape(B, T, H, v.shape[-1])
    return o, S_fin


def rglru_branch(xb, gb, conv_buf, h0, first, conv_w, conv_b, wa, ba, wx, bx, lam):
    B, T, _ = xb.shape
    xp = jnp.concatenate([conv_buf.astype(xb.dtype), xb], axis=1)
    conv = conv_b + sum(xp[:, j:j + T] * conv_w[j] for j in range(CONV_W))
    new_buf = xp[:, T:]
    cf = conv.astype(jnp.float32)
    cb = cf.reshape(B, T, LRU_BLOCKS, LRU_BW)
    r = jax.nn.sigmoid(jnp.einsum('btni,nij->btnj', cb, wa) + ba).reshape(B, T, LRU_W)
    i = jax.nn.sigmoid(jnp.einsum('btni,nij->btnj', cb, wx) + bx).reshape(B, T, LRU_W)
    log_a = LRU_C * r * jax.nn.log_sigmoid(lam)
    reset = first[None, :, None]
    mult = jnp.where(reset, 1.0, jnp.sqrt(-jnp.expm1(2.0 * log_a)))
    a = jnp.where(reset, 0.0, jnp.exp(log_a))
    u = cf * i * mult
    u = u.at[:, 0].add(a[:, 0] * h0.astype(jnp.float32))

    def combine(left, right):
        return (left[0] * right[0], right[0] * left[1] + right[1])

    _, h = lax.associative_scan(combine, (a, u), axis=1)
    y = jax.nn.gelu(gb.astype(jnp.float32)) * h
    return y.astype(xb.dtype), new_buf, h[:, -1]


def moe(x, w_router, b_router, w_gu, b_gu, w_down, b_down):
    B, T, D = x.shape
    n_tok = B * T
    xf = x.reshape(n_tok, D)
    logits = (xf @ w_router + b_router).astype(jnp.float32)
    top_val, top_idx = lax.top_k(logits, TOP_K)
    gates = jax.nn.softmax(top_val, axis=-1)
    n_slots = n_tok * TOP_K
    e_flat = top_idx.reshape(n_slots)
    tok_flat = jnp.repeat(jnp.arange(n_tok, dtype=jnp.int32), TOP_K)
    order = jnp.argsort(e_flat)
    e_sorted = e_flat[order]
    tok_sorted = tok_flat[order]
    gate_sorted = gates.reshape(n_slots)[order]
    counts = jnp.bincount(e_flat, length=N_EXPERTS)
    padded = (counts + MOE_BLOCK - 1) // MOE_BLOCK * MOE_BLOCK
    start = jnp.cumsum(counts) - counts
    end_pad = jnp.cumsum(padded)
    start_pad = end_pad - padded
    dest = start_pad[e_sorted] + jnp.arange(n_slots, dtype=jnp.int32) - start[e_sorted]
    n_blocks = -(-n_slots // MOE_BLOCK) + N_EXPERTS
    n_rows = n_blocks * MOE_BLOCK
    row_tok = jnp.full((n_rows,), n_tok, jnp.int32).at[dest].set(tok_sorted)
    row_gate = jnp.zeros((n_rows,), jnp.float32).at[dest].set(gate_sorted)
    block_expert = jnp.minimum(
        jnp.searchsorted(end_pad, jnp.arange(n_blocks) * MOE_BLOCK, side='right'), N_EXPERTS - 1)
    x_pad = jnp.concatenate([xf, jnp.zeros((1, D), xf.dtype)], axis=0)
    xs = x_pad[row_tok].reshape(n_blocks, MOE_BLOCK, D)
    gs = row_gate.reshape(n_blocks, MOE_BLOCK)

    def expert_block(args):
        xb, e, gb = args
        hgu = xb @ w_gu[e] + b_gu[e]
        g = jnp.minimum(hgu[:, :D_FF], SWIGLU_LIMIT)
        u = jnp.clip(hgu[:, D_FF:], -SWIGLU_LIMIT, SWIGLU_LIMIT)
        hid = (u + 1.0) * g * jax.nn.sigmoid(SWIGLU_ALPHA * g)
        y = (hid @ w_down[e] + b_down[e]).astype(jnp.float32)
        return y * gb[:, None]

    ys = lax.map(expert_block, (xs, block_expert, gs))
    out = jnp.zeros((n_tok + 1, D), jnp.float32).at[row_tok].add(ys.reshape(n_rows, D))
    return out[:n_tok].reshape(B, T, D).astype(x.dtype)


def block(x, pos, past_k, past_v, S0, h0, conv_buf, lb,
          w_in, sb_bias, hg_norm_g, lru_conv_w, lru_conv_b, lru_wa, lru_ba, lru_wx, lru_bx, lru_lam,
          w_out, ln1_g, ln1_b, w_router, b_router, w_gu, b_gu, w_down, b_down, ln2_g, ln2_b):
    B, T, _ = x.shape
    proj = x @ w_in
    q, k, v, hq, hf, hi, hg, lx, lg = split_cols(proj, IN_WIDTHS)

    def heads(a, nh):
        return a.reshape(B, T, nh, HEAD_DIM)

    q, k, v = heads(q, ATT_HEADS), heads(k, ATT_HEADS), heads(v, ATT_HEADS)
    if past_k is None:
        nb = T // Q_BLOCK
        qb = q.reshape(B, nb, Q_BLOCK, ATT_HEADS, HEAD_DIM).transpose(1, 0, 2, 3, 4)
        pb = pos.reshape(nb, Q_BLOCK)
        ob = lax.map(lambda a: stick_breaking(a[0], k, v, a[1], pos, sb_bias), (qb, pb))
        att = ob.transpose(1, 0, 2, 3, 4).reshape(B, T, ATT_W)
    else:
        k_all = jnp.concatenate([past_k.astype(k.dtype), k], axis=1)
        v_all = jnp.concatenate([past_v.astype(v.dtype), v], axis=1)
        k_pos = jnp.arange(k_all.shape[1], dtype=jnp.int32)
        att = stick_breaking(q, k_all, v_all, pos, k_pos, sb_bias).reshape(B, T, ATT_W)

    hf32 = hf.astype(jnp.float32)
    log_f = jnp.logaddexp(jnp.log(lb), jnp.log1p(-lb) + jax.nn.log_sigmoid(hf32))
    hk = (1.0 - lb) * jax.nn.sigmoid(-hf32)
    o, S_new = hgrn2_mix(heads(hq, HG_HEADS), heads(hk, HG_HEADS), heads(hi, HG_HEADS),
                         heads(log_f, HG_HEADS), S0)
    o = o * lax.rsqrt(jnp.mean(jnp.square(o), -1, keepdims=True) + RMS_EPS)
    hgrn = (o.reshape(B, T, HG_W) * hg_norm_g * jax.nn.silu(hg.astype(jnp.float32))).astype(x.dtype)

    lru, buf_new, h_new = rglru_branch(lx, lg, conv_buf, h0, pos == 0, lru_conv_w, lru_conv_b,
                                       lru_wa, lru_ba, lru_wx, lru_bx, lru_lam)

    mix = jnp.concatenate([att, hgrn, lru], axis=-1) @ w_out
    x = layer_norm(DN_ALPHA * x + mix, ln1_g, ln1_b)
    x = layer_norm(DN_ALPHA * x + moe(x, w_router, b_router, w_gu, b_gu, w_down, b_down), ln2_g, ln2_b)
    return x, k, v, S_new.astype(x.dtype), h_new.astype(x.dtype), buf_new


def setup_inputs(seed: int = 0) -> dict:
    key = jax.random.key(seed)
    ks = jax.random.split(key, 32)
    f32 = jnp.float32

    def nrm(k, shape, scale):
        return jax.random.normal(k, shape, f32) * scale

    n_pages = PAST_LEN // PAGE_SIZE
    n_used = DEC_BATCH * n_pages
    n_pool = n_used + max(1, n_used // 4)
    page_table = jax.random.permutation(ks[4], n_pool)[:n_used].reshape(DEC_BATCH, n_pages).astype(jnp.int32)
    radius = jnp.sqrt(jax.random.uniform(ks[18], (DEPTH, LRU_W), f32, 0.81, 0.998))
    return {
        'x_prompt': nrm(ks[0], (BATCH, SEQ, D_MODEL), 1.0),
        'x_sample': nrm(ks[1], (DEC_BATCH, DEC_SEQ, D_MODEL), 1.0),
        'cache_k': nrm(ks[2], (DEPTH, n_pool, PAGE_SIZE, ATT_HEADS, HEAD_DIM), 1.0),
        'cache_v': nrm(ks[3], (DEPTH, n_pool, PAGE_SIZE, ATT_HEADS, HEAD_DIM), 1.0),
        'page_table': page_table,
        'state_hgrn': nrm(ks[5], (DEPTH, DEC_BATCH, HG_HEADS, HEAD_DIM, HEAD_DIM), 0.5),
        'state_lru': nrm(ks[6], (DEPTH, DEC_BATCH, LRU_W), 1.0),
        'state_conv': nrm(ks[7], (DEPTH, DEC_BATCH, CONV_W - 1, LRU_W), 1.0),
        'ln_in_g': 1.0 + nrm(ks[8], (D_MODEL,), 0.05),
        'ln_in_b': nrm(ks[9], (D_MODEL,), 0.02),
        'w_in': nrm(ks[10], (DEPTH, D_MODEL, IN_COLS), D_MODEL ** -0.5),
        'sb_bias': SB_BIAS_INIT + nrm(ks[31], (DEPTH, ATT_HEADS), 0.3),
        'hg_lb': nrm(ks[11], (DEPTH, HG_W), 0.5),
        'hg_norm_g': 1.0 + nrm(ks[12], (DEPTH, HG_W), 0.05),
        'lru_conv_w': nrm(ks[13], (DEPTH, CONV_W, LRU_W), CONV_W ** -0.5),
        'lru_conv_b': nrm(ks[14], (DEPTH, LRU_W), 0.02),
        'lru_wa': nrm(ks[15], (DEPTH, LRU_BLOCKS, LRU_BW, LRU_BW), LRU_BW ** -0.5),
        'lru_ba': nrm(ks[16], (DEPTH, LRU_BLOCKS, LRU_BW), 0.02),
        'lru_wx': nrm(ks[17], (DEPTH, LRU_BLOCKS, LRU_BW, LRU_BW), LRU_BW ** -0.5),
        'lru_bx': nrm(ks[19], (DEPTH, LRU_BLOCKS, LRU_BW), 0.02),
        'lru_lam': jnp.log(radius) - jnp.log1p(-radius),
        'w_out': nrm(ks[20], (DEPTH, MIX_W, D_MODEL), MIX_W ** -0.5 * DN_BETA),
        'ln1_g': 1.0 + nrm(ks[21], (DEPTH, D_MODEL), 0.05),
        'ln1_b': nrm(ks[22], (DEPTH, D_MODEL), 0.02),
        'w_router': nrm(ks[23], (DEPTH, D_MODEL, N_EXPERTS), D_MODEL ** -0.5),
        'b_router': nrm(ks[24], (DEPTH, N_EXPERTS), 0.01),
        'w_gu': nrm(ks[25], (DEPTH, N_EXPERTS, D_MODEL, 2 * D_FF), D_MODEL ** -0.5),
        'b_gu': nrm(ks[26], (DEPTH, N_EXPERTS, 2 * D_FF), 0.02),
        'w_down': nrm(ks[27], (DEPTH, N_EXPERTS, D_FF, D_MODEL), D_FF ** -0.5 * DN_BETA),
        'b_down': nrm(ks[28], (DEPTH, N_EXPERTS, D_MODEL), 0.02 * DN_BETA),
        'ln2_g': 1.0 + nrm(ks[29], (DEPTH, D_MODEL), 0.05),
        'ln2_b': nrm(ks[30], (DEPTH, D_MODEL), 0.02),
    }


def reference(x_prompt, x_sample, cache_k, cache_v, page_table, state_hgrn, state_lru, state_conv,
              ln_in_g, ln_in_b, w_in, sb_bias, hg_lb, hg_norm_g, lru_conv_w, lru_conv_b, lru_wa, lru_ba,
              lru_wx, lru_bx, lru_lam, w_out, ln1_g, ln1_b, w_router, b_router, w_gu, b_gu,
              w_down, b_down, ln2_g, ln2_b):
    f32 = jnp.float32
    bp, tp, _ = x_prompt.shape
    bs, ts, _ = x_sample.shape
    past_len = page_table.shape[1] * cache_k.shape[2]
    pos_p = jnp.arange(tp, dtype=jnp.int32)
    pos_s = past_len + jnp.arange(ts, dtype=jnp.int32)
    lb_cum = jnp.cumsum(jax.nn.softmax(hg_lb.astype(f32), axis=0), axis=0)
    lb_all = lb_cum - lb_cum[0]
    hp = layer_norm(x_prompt, ln_in_g, ln_in_b)
    hs = layer_norm(x_sample, ln_in_g, ln_in_b)
    new_p = []
    new_s = []
    for l in range(DEPTH):
        weights = (w_in[l], sb_bias[l], hg_norm_g[l], lru_conv_w[l], lru_conv_b[l], lru_wa[l], lru_ba[l],
                   lru_wx[l], lru_bx[l], lru_lam[l], w_out[l], ln1_g[l], ln1_b[l],
                   w_router[l], b_router[l], w_gu[l], b_gu[l], w_down[l], b_down[l], ln2_g[l], ln2_b[l])
        hp, kp, vp, sp, lp, cp = block(hp, pos_p, None, None,
                                       jnp.zeros((bp, HG_HEADS, HEAD_DIM, HEAD_DIM), f32),
                                       jnp.zeros((bp, LRU_W), f32),
                                       jnp.zeros((bp, CONV_W - 1, LRU_W), x_prompt.dtype),
                                       lb_all[l], *weights)
        past_k = cache_k[l][page_table].reshape(bs, past_len, ATT_HEADS, HEAD_DIM)
        past_v = cache_v[l][page_table].reshape(bs, past_len, ATT_HEADS, HEAD_DIM)
        hs, k_s, v_s, s_s, l_s, c_s = block(hs, pos_s, past_k, past_v, state_hgrn[l], state_lru[l],
                                            state_conv[l], lb_all[l], *weights)
        new_p.append((kp, vp, sp, lp, cp))
        new_s.append((k_s, v_s, s_s, l_s, c_s))

    def stacked(rows, i):
        return jnp.stack([r[i] for r in rows])

    return (hp, hs,
            stacked(new_p, 0), stacked(new_p, 1), stacked(new_p, 2), stacked(new_p, 3), stacked(new_p, 4),
            stacked(new_s, 0), stacked(new_s, 1), stacked(new_s, 2), stacked(new_s, 3), stacked(new_s, 4))
```

```python
import functools

import jax
import jax.numpy as jnp
from jax import lax
from jax.experimental import pallas as pl
from jax.experimental.pallas import tpu as pltpu

F32 = jnp.float32
BF16 = jnp.bfloat16
HIGHEST = lax.Precision.HIGHEST

HEAD_DIM = 64
LANES = 128
CONV_W = 4
LRU_C = 8.0
N_EXPERTS = 32
TOP_K = 4
SWIGLU_LIMIT = 7.0
SWIGLU_ALPHA = 1.702
LN_EPS = 1e-5
RMS_EPS = 1e-6

ROW_TILE = 256
ATT_TILE = 256
HG_CHUNK = 64
HG_SUB = 16
HG_STEP_ROWS = 512
LRU_STEP_ROWS = 512
MOE_ROWS = 256
VMEM_LIMIT = 48 * 2**20

NT_DIMS = (((1,), (1,)), ((), ()))
TN_DIMS = (((0,), (0,)), ((), ()))


def _params(*sem):
    return pltpu.CompilerParams(dimension_semantics=sem, vmem_limit_bytes=VMEM_LIMIT)


def _dot(a, b, precision=None):
    return jnp.dot(a, b, preferred_element_type=F32, precision=precision)


def _dot_nt(a, b):
    return lax.dot_general(a, b, NT_DIMS, preferred_element_type=F32)


def _sigmoid(x):
    return 1.0 / (1.0 + jnp.exp(-x))


def _log_sigmoid(x):
    return jnp.minimum(x, 0.0) - jnp.log1p(jnp.exp(-jnp.abs(x)))


def _layer_norm(x, g, b):
    mu = jnp.mean(x, axis=-1, keepdims=True)
    xc = x - mu
    var = jnp.mean(xc * xc, axis=-1, keepdims=True)
    return xc * lax.rsqrt(var + LN_EPS) * g + b


def _gelu_tanh(x):
    return 0.5 * x * (1.0 + jnp.tanh(0.7978845608028654 * (x + 0.044715 * (x * x * x))))


def _ln_kernel(x_ref, g_ref, b_ref, o_ref):
    o_ref[...] = _layer_norm(x_ref[...], g_ref[...], b_ref[...])


def layer_norm_rows(x, g, b):
    n, d = x.shape
    return pl.pallas_call(
        _ln_kernel,
        out_shape=jax.ShapeDtypeStruct((n, d), F32),
        grid=(n // ROW_TILE,),
        in_specs=[pl.BlockSpec((ROW_TILE, d), lambda i: (i, 0)),
                  pl.BlockSpec((1, d), lambda i: (0, 0)),
                  pl.BlockSpec((1, d), lambda i: (0, 0))],
        out_specs=pl.BlockSpec((ROW_TILE, d), lambda i: (i, 0)),
        name="ln_in",
        compiler_params=_params("parallel"),
    )(x, g.reshape(1, d), b.reshape(1, d))


def _inproj_kernel(x_ref, w_ref, o_ref):
    o_ref[...] = _dot(x_ref[...].astype(BF16), w_ref[...])


def in_proj(x, w_bf16):
    n, d = x.shape
    cols = w_bf16.shape[1]
    return pl.pallas_call(
        _inproj_kernel,
        out_shape=jax.ShapeDtypeStruct((n, cols), F32),
        grid=(n // ROW_TILE,),
        in_specs=[pl.BlockSpec((ROW_TILE, d), lambda i: (i, 0)),
                  pl.BlockSpec((d, cols), lambda i: (0, 0))],
        out_specs=pl.BlockSpec((ROW_TILE, cols), lambda i: (i, 0)),
        name="in_proj",
        compiler_params=_params("parallel"),
    )(x, w_bf16)


def _attn_prompt_kernel(bias_ref, q_ref, k_ref, v_ref, o_ref, acc_ref, c_ref, *, tile):
    hp = pl.program_id(1)
    i = pl.program_id(2)
    lane = lax.broadcasted_iota(jnp.int32, (1, LANES), 1)
    head_lanes = (lane < HEAD_DIM, lane >= HEAD_DIM)
    q = q_ref[...] * (HEAD_DIM ** -0.5)
    qh = [jnp.where(m, q, 0.0).astype(BF16) for m in head_lanes]
    bias = [bias_ref[pl.ds(2 * hp + a, 1), :] for a in range(2)]
    row = lax.broadcasted_iota(jnp.int32, (tile, tile), 0)
    col = lax.broadcasted_iota(jnp.int32, (tile, tile), 1)
    later = jnp.where(row > col, 1.0, 0.0).astype(BF16)
    causal = col < row
    acc_ref[...] = jnp.zeros_like(acc_ref)
    c_ref[...] = jnp.zeros_like(c_ref)

    def key_tile(j, diag):
        start = pl.multiple_of(j * tile, tile)
        kb = k_ref[pl.ds(start, tile), :].astype(BF16)
        vb = v_ref[pl.ds(start, tile), :].astype(BF16)
        for a in range(2):
            z = _dot_nt(qh[a], kb) + bias[a]
            log_stay = -(jnp.maximum(z, 0.0) + jnp.log1p(jnp.exp(-jnp.abs(z))))
            log_break = z + log_stay
            if diag:
                log_stay = jnp.where(causal, log_stay, 0.0)
            after = _dot(log_stay.astype(BF16), later) + c_ref[a]
            w = jnp.exp(log_break + after)
            if diag:
                w = jnp.where(causal, w, 0.0)
            acc_ref[a] += _dot(w.astype(BF16), vb)
            c_ref[a] += jnp.sum(log_stay, axis=1, keepdims=True)

    key_tile(i, True)

    def body(it, carry):
        key_tile(i - 1 - it, False)
        return carry

    lax.fori_loop(0, i, body, 0)
    o_ref[...] = jnp.where(head_lanes[0], acc_ref[0], acc_ref[1]).astype(o_ref.dtype)


def attn_prompt(proj, sb_bias, *, batch, seq, col_q, col_k, col_v, n_heads):
    tile = min(ATT_TILE, seq)
    nq = seq // tile
    pairs = n_heads // 2
    bias_b = jnp.broadcast_to(sb_bias.astype(F32)[:, None], (n_heads, tile))
    kernel = functools.partial(_attn_prompt_kernel, tile=tile)
    return pl.pallas_call(
        kernel,
        out_shape=jax.ShapeDtypeStruct((batch * seq, n_heads * HEAD_DIM), BF16),
        grid=(batch, pairs, nq),
        in_specs=[pl.BlockSpec((n_heads, tile), lambda b, hp, i: (0, 0)),
                  pl.BlockSpec((tile, LANES), lambda b, hp, i: (b * nq + i, col_q // LANES + hp)),
                  pl.BlockSpec((seq, LANES), lambda b, hp, i: (b, col_k // LANES + hp)),
                  pl.BlockSpec((seq, LANES), lambda b, hp, i: (b, col_v // LANES + hp))],
        out_specs=pl.BlockSpec((tile, LANES), lambda b, hp, i: (b * nq + i, hp)),
        scratch_shapes=[pltpu.VMEM((2, tile, LANES), F32), pltpu.VMEM((2, tile, 1), F32)],
        name="attn_prompt",
        compiler_params=_params("parallel", "parallel", "arbitrary"),
    )(bias_b, proj, proj, proj)


def _attn_decode_kernel(pt_ref, q_ref, bias_ref, *refs, n_pages, page, n_heads):
    del pt_ref
    k_refs = refs[:n_pages]
    v_refs = refs[n_pages:2 * n_pages]
    o_ref = refs[2 * n_pages]
    z_ref = refs[2 * n_pages + 1]
    b = pl.program_id(0)
    width = n_heads * HEAD_DIM
    rows = n_pages * n_heads

    @pl.when(b == 0)
    def _():
        o_ref[...] = jnp.zeros_like(o_ref)

    head_row = lax.broadcasted_iota(jnp.int32, (n_heads, width), 0)
    head_lane = lax.broadcasted_iota(jnp.int32, (n_heads, width), 1) // HEAD_DIM
    own = head_row == head_lane
    q = q_ref[pl.ds(b, 1), :] * (HEAD_DIM ** -0.5)
    qbd = jnp.where(own, q, 0.0).astype(BF16)
    for p in range(n_pages):
        z_ref[p * n_heads:(p + 1) * n_heads, :] = _dot_nt(qbd, k_refs[p][...].astype(BF16))
    z = z_ref[...] + bias_ref[...]
    log_stay = -(jnp.maximum(z, 0.0) + jnp.log1p(jnp.exp(-jnp.abs(z))))
    r = lax.broadcasted_iota(jnp.int32, (page, page), 0)
    c = lax.broadcasted_iota(jnp.int32, (page, page), 1)
    later_key = jnp.where(r > c, 1.0, 0.0)
    rr = lax.broadcasted_iota(jnp.int32, (rows, rows), 0)
    cc = lax.broadcasted_iota(jnp.int32, (rows, rows), 1)
    later_page = jnp.where((rr % n_heads == cc % n_heads) & (cc > rr), 1.0, 0.0)
    within = _dot(log_stay, later_key, HIGHEST)
    page_total = jnp.sum(log_stay, axis=1, keepdims=True)
    across = _dot(later_page, jnp.broadcast_to(page_total, (rows, page)), HIGHEST)
    w = jnp.exp(z + log_stay + within + across)
    acc = jnp.zeros((n_heads, width), F32)
    for p in range(n_pages):
        acc += _dot(w[p * n_heads:(p + 1) * n_heads, :].astype(BF16), v_refs[p][...].astype(BF16))
    o_ref[pl.ds(b, 1), :] = jnp.sum(jnp.where(own, acc, 0.0), axis=0, keepdims=True)


def attn_decode(q_rows, sb_bias, cache_k, cache_v, page_table, layer, *, out_rows):
    n_seq, width = q_rows.shape
    n_heads = width // HEAD_DIM
    n_pages = page_table.shape[1]
    page = cache_k.shape[2]
    rows = n_pages * n_heads
    bias_col = jnp.tile(sb_bias.astype(F32), n_pages)[:, None]

    def page_spec(p):
        return pl.BlockSpec((None, None, page, width), lambda b, pt: (layer, pt[b, p], 0, 0))

    kernel = functools.partial(_attn_decode_kernel, n_pages=n_pages, page=page, n_heads=n_heads)
    return pl.pallas_call(
        kernel,
        out_shape=jax.ShapeDtypeStruct((out_rows, width), F32),
        grid_spec=pltpu.PrefetchScalarGridSpec(
            num_scalar_prefetch=1,
            grid=(n_seq,),
            in_specs=([pl.BlockSpec((n_seq, width), lambda b, pt: (0, 0)),
                       pl.BlockSpec((rows, 1), lambda b, pt: (0, 0))]
                      + [page_spec(p) for p in range(n_pages)]
                      + [page_spec(p) for p in range(n_pages)]),
            out_specs=pl.BlockSpec((out_rows, width), lambda b, pt: (0, 0)),
            scratch_shapes=[pltpu.VMEM((rows, page), F32)]),
        name="attn_decode",
        compiler_params=_params("arbitrary"),
    )(page_table, q_rows, bias_col, *([cache_k] * n_pages), *([cache_v] * n_pages))


def _hgrn_gates(hf, lb):
    a1 = jnp.log(lb)
    a2 = jnp.log1p(-lb) + _log_sigmoid(hf)
    log_f = jnp.maximum(a1, a2) + jnp.log1p(jnp.exp(-jnp.abs(a1 - a2)))
    key = (1.0 - lb) * (1.0 / (1.0 + jnp.exp(hf)))
    return log_f, key


def _head_block_mask(n):
    r = lax.broadcasted_iota(jnp.int32, (n, n), 0) // HEAD_DIM
    c = lax.broadcasted_iota(jnp.int32, (n, n), 1) // HEAD_DIM
    return r == c


def _hgrn_finish(o, gate_pre, norm_g, same_head):
    ms = _dot(o * o, jnp.where(same_head, 1.0 / HEAD_DIM, 0.0), HIGHEST)
    return o * lax.rsqrt(ms + RMS_EPS) * norm_g * (gate_pre * _sigmoid(gate_pre))


def _hgrn_prompt_kernel(lb_ref, g_ref, hq_ref, hf_ref, hi_ref, hg_ref, o_ref, st_out_ref, st_ref, *, chunks):
    @pl.when(pl.program_id(2) == 0)
    def _():
        st_ref[...] = jnp.zeros_like(st_ref)

    C, S = HG_CHUNK, HG_SUB
    lane = lax.broadcasted_iota(jnp.int32, (1, LANES), 1)
    head_lanes = (lane < HEAD_DIM, lane >= HEAD_DIM)
    same_head = _head_block_mask(LANES)
    ones_head = jnp.where(same_head, 1.0, 0.0).astype(BF16)
    tri = jnp.where(lax.broadcasted_iota(jnp.int32, (C, C), 0) >= lax.broadcasted_iota(jnp.int32, (C, C), 1),
                    1.0, 0.0)
    key_pos = lax.broadcasted_iota(jnp.int32, (S, C), 1)
    sub_row = lax.broadcasted_iota(jnp.int32, (S, 1), 0)
    lb = lb_ref[...]
    norm_g = g_ref[...]

    def chunk(ci, carry):
        rows = pl.ds(pl.multiple_of(ci * C, C), C)
        q = hq_ref[rows, :]
        v = hi_ref[rows, :]
        log_f, k = _hgrn_gates(hf_ref[rows, :], lb)
        b = _dot(tri, log_f, HIGHEST)
        st = st_ref[...]
        vb = v.astype(BF16)
        o = _dot_nt((q * jnp.exp(b)).astype(BF16), st.astype(BF16))

        att = [[jnp.zeros((S, C), F32)], [jnp.zeros((S, C), F32)]]
        for s0 in range(S, C, S):
            ref = b[s0 - 1:s0, :]
            qt = q[s0:s0 + S, :] * jnp.exp(b[s0:s0 + S, :] - ref)
            kt = (k * jnp.exp(jnp.minimum(ref - b, 0.0))).astype(BF16)
            for a in range(2):
                blk = _dot_nt(jnp.where(head_lanes[a], qt, 0.0).astype(BF16), kt)
                att[a].append(jnp.where(key_pos < s0, blk, 0.0))
        far = [_dot(jnp.concatenate(att[a], axis=0).astype(BF16), vb) for a in range(2)]
        o = o + jnp.where(head_lanes[0], far[0], far[1])

        near = []
        for s0 in range(0, C, S):
            qs = q[s0:s0 + S, :]
            bs = b[s0:s0 + S, :]
            terms = []
            for sl in range(S):
                s = s0 + sl
                decay = jnp.exp(jnp.where(sub_row >= sl, bs - b[s:s + 1, :], -jnp.inf))
                terms.append((qs * k[s:s + 1, :] * decay).astype(BF16))
            wgt = _dot(jnp.concatenate(terms, axis=0), ones_head)
            acc = jnp.zeros((S, LANES), F32)
            for sl in range(S):
                acc = acc + wgt[sl * S:(sl + 1) * S, :] * v[s0 + sl:s0 + sl + 1, :]
            near.append(acc)
        o = o + jnp.concatenate(near, axis=0)

        b_last = b[C - 1:C, :]
        upd = lax.dot_general(vb, (k * jnp.exp(b_last - b)).astype(BF16), TN_DIMS, preferred_element_type=F32)
        st_ref[...] = st * jnp.exp(b_last) + jnp.where(same_head, upd, 0.0)
        o_ref[rows, :] = _hgrn_finish(o, hg_ref[rows, :], norm_g, same_head).astype(o_ref.dtype)
        return carry

    lax.fori_loop(0, chunks, chunk, 0)
    st_out_ref[...] = st_ref[...]


def hgrn_prompt(proj, lb, norm_g, *, batch, seq, col_q, col_f, col_i, col_g, n_heads):
    step = min(HG_STEP_ROWS, seq)
    ns = seq // step
    pairs = n_heads // 2
    width = n_heads * HEAD_DIM

    def col_spec(col):
        return pl.BlockSpec((step, LANES), lambda b, hp, i: (b * ns + i, col // LANES + hp))

    kernel = functools.partial(_hgrn_prompt_kernel, chunks=step // HG_CHUNK)
    return pl.pallas_call(
        kernel,
        out_shape=(jax.ShapeDtypeStruct((batch * seq, width), BF16),
                   jax.ShapeDtypeStruct((batch, pairs, LANES, LANES), F32)),
        grid=(batch, pairs, ns),
        in_specs=[pl.BlockSpec((1, LANES), lambda b, hp, i: (0, hp)),
                  pl.BlockSpec((1, LANES), lambda b, hp, i: (0, hp)),
                  col_spec(col_q), col_spec(col_f), col_spec(col_i), col_spec(col_g)],
        out_specs=(pl.BlockSpec((step, LANES), lambda b, hp, i: (b * ns + i, hp)),
                   pl.BlockSpec((None, None, LANES, LANES), lambda b, hp, i: (b, hp, 0, 0))),
        scratch_shapes=[pltpu.VMEM((LANES, LANES), F32)],
        name="hgrn_prompt",
        compiler_params=_params("parallel", "parallel", "arbitrary"),
    )(lb.reshape(1, width), norm_g.reshape(1, width), proj, proj, proj, proj)


def _hgrn_decode_kernel(lb_ref, g_ref, p_ref, s_ref, o_ref, s_out_ref, pt_ref, *, n_seq, n_heads):
    h = pl.program_id(0)
    width = n_heads * HEAD_DIM
    D = HEAD_DIM

    @pl.when(h == 0)
    def _():
        o_ref[...] = jnp.zeros_like(o_ref)
        pr = p_ref[...]
        log_f, k = _hgrn_gates(pr[:, width:2 * width], lb_ref[...])
        pt_ref[0] = pr[:, 0:width].T
        pt_ref[1] = jnp.exp(log_f).T
        pt_ref[2] = k.T
        pt_ref[3] = pr[:, 2 * width:3 * width].T

    rows = pl.ds(pl.multiple_of(h * D, D), D)
    q = pt_ref[0, rows, :]
    f = pt_ref[1, rows, :]
    k = pt_ref[2, rows, :]
    v = pt_ref[3, rows, :]
    st = s_ref[...].T.reshape(D, D, n_seq)
    new = st * f[:, None, :] + k[:, None, :] * v[None, :, :]
    s_out_ref[...] = new.reshape(D * D, n_seq).T
    o_t = jnp.sum(new * q[:, None, :], axis=0)
    pt_ref[4, rows, :] = o_t

    @pl.when(h == n_heads - 1)
    def _():
        o = pt_ref[4].T
        pr = p_ref[...]
        same_head = _head_block_mask(width)
        out = _hgrn_finish(o, pr[:, 3 * width:4 * width], g_ref[...], same_head)
        o_ref[0:n_seq, :] = out.astype(o_ref.dtype)


def hgrn_decode(proj_h, state, lb, norm_g, *, out_rows):
    n_seq, n_heads = state.shape[0], state.shape[1]
    width = n_heads * HEAD_DIM
    dd = HEAD_DIM * HEAD_DIM
    kernel = functools.partial(_hgrn_decode_kernel, n_seq=n_seq, n_heads=n_heads)
    out, new_state = pl.pallas_call(
        kernel,
        out_shape=(jax.ShapeDtypeStruct((out_rows, width), BF16),
                   jax.ShapeDtypeStruct((n_seq, n_heads * dd), F32)),
        grid=(n_heads,),
        in_specs=[pl.BlockSpec((1, width), lambda h: (0, 0)),
                  pl.BlockSpec((1, width), lambda h: (0, 0)),
                  pl.BlockSpec((n_seq, 4 * width), lambda h: (0, 0)),
                  pl.BlockSpec((n_seq, dd), lambda h: (0, h))],
        out_specs=(pl.BlockSpec((out_rows, width), lambda h: (0, 0)),
                   pl.BlockSpec((n_seq, dd), lambda h: (0, h))),
        scratch_shapes=[pltpu.VMEM((5, width, n_seq), F32)],
        name="hgrn_decode",
        compiler_params=_params("arbitrary"),
    )(lb.reshape(1, width), norm_g.reshape(1, width), proj_h, state.reshape(n_seq, n_heads * dd))
    return out, new_state.reshape(state.shape)


def _lru_gates(conv, wa_ref, ba_ref, wx_ref, bx_ref, lam_ref):
    cb = conv.astype(BF16)
    r = _sigmoid(_dot(cb, wa_ref[...]) + ba_ref[...])
    i = _sigmoid(_dot(cb, wx_ref[...]) + bx_ref[...])
    log_a = LRU_C * r * _log_sigmoid(lam_ref[...])
    th = jnp.tanh(log_a)
    mult = jnp.sqrt(-2.0 * th / (1.0 - th))
    return jnp.exp(log_a), conv * i, mult


def _lru_prompt_kernel(x_ref, g_ref, cw_ref, cb_ref, wa_ref, ba_ref, wx_ref, bx_ref, lam_ref,
                       y_ref, hl_ref, xbuf, a_scr, u_scr, h_scr, hc_scr, *, step):
    t = pl.program_id(1)

    @pl.when(t == 0)
    def _():
        xbuf[0:8, :] = jnp.zeros((8, xbuf.shape[1]), F32)
        hc_scr[...] = jnp.zeros_like(hc_scr)

    @pl.when(t > 0)
    def _():
        xbuf[0:8, :] = xbuf[step:step + 8, :]

    xbuf[8:step + 8, :] = x_ref[...]
    conv = cb_ref[...]
    for j in range(CONV_W):
        off = 8 - (CONV_W - 1) + j
        conv = conv + xbuf[off:off + step, :] * cw_ref[j:j + 1, :]
    a, ci, mult = _lru_gates(conv, wa_ref, ba_ref, wx_ref, bx_ref, lam_ref)
    first = (t * step + lax.broadcasted_iota(jnp.int32, (step, 1), 0)) == 0
    a_scr[...] = jnp.where(first, 0.0, a)
    u_scr[...] = ci * jnp.where(first, 1.0, mult)
    rid = lax.broadcasted_iota(jnp.int32, (8, 1), 0)

    def group(gi, hc):
        rows = pl.ds(pl.multiple_of(gi * 8, 8), 8)
        a8 = a_scr[rows, :]
        u8 = u_scr[rows, :]
        for sh in (1, 2, 4):
            ok = rid >= sh
            u8 = jnp.where(ok, a8 * pltpu.roll(u8, sh, 0) + u8, u8)
            a8 = jnp.where(ok, a8 * pltpu.roll(a8, sh, 0), a8)
        h8 = a8 * hc + u8
        h_scr[rows, :] = h8
        return h8[7:8, :]

    hc = lax.fori_loop(0, step // 8, group, hc_scr[...])
    hc_scr[...] = hc
    hl_ref[...] = hc
    y_ref[...] = (_gelu_tanh(g_ref[...]) * h_scr[...]).astype(y_ref.dtype)


def _block_diag(w):
    nb, bw, _ = w.shape
    out = jnp.zeros((nb * bw, nb * bw), w.dtype)
    for n in range(nb):
        out = out.at[n * bw:(n + 1) * bw, n * bw:(n + 1) * bw].set(w[n])
    return out


def lru_prompt(proj, weights, *, batch, seq, col_x, col_g, width):
    step = min(LRU_STEP_ROWS, seq)
    ns = seq // step
    cw, cb, wa, ba, wx, bx, lam = weights
    vec = pl.BlockSpec((1, width), lambda b, i: (0, 0))
    mat = pl.BlockSpec((width, width), lambda b, i: (0, 0))
    kernel = functools.partial(_lru_prompt_kernel, step=step)
    return pl.pallas_call(
        kernel,
        out_shape=(jax.ShapeDtypeStruct((batch * seq, width), BF16),
                   jax.ShapeDtypeStruct((batch, 1, width), F32)),
        grid=(batch, ns),
        in_specs=[pl.BlockSpec((step, width), lambda b, i: (b * ns + i, col_x // width)),
                  pl.BlockSpec((step, width), lambda b, i: (b * ns + i, col_g // width)),
                  pl.BlockSpec((CONV_W, width), lambda b, i: (0, 0)), vec, mat, vec, mat, vec, vec],
        out_specs=(pl.BlockSpec((step, width), lambda b, i: (b * ns + i, 0)),
                   pl.BlockSpec((None, 1, width), lambda b, i: (b, 0, 0))),
        scratch_shapes=[pltpu.VMEM((step + 8, width), F32), pltpu.VMEM((step, width), F32),
                        pltpu.VMEM((step, width), F32), pltpu.VMEM((step, width), F32),
                        pltpu.VMEM((1, width), F32)],
        name="lru_prompt",
        compiler_params=_params("parallel", "arbitrary"),
    )(proj, proj, cw, cb, wa, ba, wx, bx, lam)


def _lru_decode_kernel(x_ref, g_ref, b0_ref, b1_ref, b2_ref, h0_ref, cw_ref, cb_ref, wa_ref, ba_ref, wx_ref,
                       bx_ref, lam_ref, y_ref, h_ref, *, n_seq):
    conv = (cb_ref[...] + b0_ref[...] * cw_ref[0:1, :] + b1_ref[...] * cw_ref[1:2, :]
            + b2_ref[...] * cw_ref[2:3, :] + x_ref[...] * cw_ref[3:4, :])
    a, ci, mult = _lru_gates(conv, wa_ref, ba_ref, wx_ref, bx_ref, lam_ref)
    h = a * h0_ref[...] + ci * mult
    h_ref[...] = h
    y_ref[...] = jnp.zeros_like(y_ref)
    y_ref[0:n_seq, :] = (_gelu_tanh(g_ref[...]) * h).astype(y_ref.dtype)


def lru_decode(x, g, conv_buf, h0, weights, *, out_rows):
    n_seq, width = x.shape
    cw, cb, wa, ba, wx, bx, lam = weights
    kernel = functools.partial(_lru_decode_kernel, n_seq=n_seq)
    return pl.pallas_call(
        kernel,
        out_shape=(jax.ShapeDtypeStruct((out_rows, width), BF16),
                   jax.ShapeDtypeStruct((n_seq, width), F32)),
        name="lru_decode",
    )(x, g, conv_buf[:, 0], conv_buf[:, 1], conv_buf[:, 2], h0, cw, cb, wa, ba, wx, bx, lam)


def _outproj_kernel(x_ref, att_p, hg_p, lru_p, att_s, hg_s, lru_s, w_ref, g_ref, b_ref, wr_ref, br_ref,
                    x1_ref, x1b_ref, idx_ref, gate_ref, *, prompt_tiles, alpha, widths):
    is_sample = pl.program_id(0) >= prompt_tiles
    wa, wh, wl = widths
    att = jnp.where(is_sample, att_s[...].astype(BF16), att_p[...])
    hg = jnp.where(is_sample, hg_s[...], hg_p[...])
    lru = jnp.where(is_sample, lru_s[...], lru_p[...])
    mix = (_dot(att, w_ref[0:wa, :]) + _dot(hg, w_ref[wa:wa + wh, :]) + _dot(lru, w_ref[wa + wh:wa + wh + wl, :]))
    x1 = _layer_norm(alpha * x_ref[...] + mix, g_ref[...], b_ref[...])
    x1_ref[...] = x1
    x1b_ref[...] = x1.astype(BF16)

    logits = _dot(x1, wr_ref[...], HIGHEST) + br_ref[...]
    rows, n_exp = logits.shape
    e_id = lax.broadcasted_iota(jnp.int32, (rows, n_exp), 1).astype(F32)
    out_lane = lax.broadcasted_iota(jnp.int32, (rows, LANES), 1)
    work = logits
    vals, idx_out = [], jnp.zeros((rows, LANES), F32)
    for k in range(TOP_K):
        m = jnp.max(work, axis=1, keepdims=True)
        ix = jnp.min(jnp.where(work == m, e_id, float(n_exp)), axis=1, keepdims=True)
        vals.append(m)
        idx_out = jnp.where(out_lane == k, ix, idx_out)
        work = jnp.where(e_id == ix, -jnp.inf, work)
    ex = [jnp.exp(v - vals[0]) for v in vals]
    inv = 1.0 / (ex[0] + ex[1] + ex[2] + ex[3])
    gate_out = jnp.zeros((rows, LANES), F32)
    for k in range(TOP_K):
        gate_out = jnp.where(out_lane == k, ex[k] * inv, gate_out)
    idx_ref[...] = idx_out.astype(jnp.int32)
    gate_ref[...] = gate_out


def out_proj_ln_router(x, prompt_mix, sample_mix, w_out_bf16, ln_g, ln_b, w_router, b_router, *, n_prompt, alpha):
    n, d = x.shape
    prompt_tiles = n_prompt // ROW_TILE
    widths = tuple(a.shape[1] for a in prompt_mix)
    last = prompt_tiles - 1

    def p_spec(wd):
        return pl.BlockSpec((ROW_TILE, wd), lambda i: (jnp.minimum(i, last), 0))

    def s_spec(wd):
        return pl.BlockSpec((ROW_TILE, wd), lambda i: (0, 0))

    def full(shape):
        return pl.BlockSpec(shape, lambda i: (0, 0))

    row_d = pl.BlockSpec((ROW_TILE, d), lambda i: (i, 0))
    row_l = pl.BlockSpec((ROW_TILE, LANES), lambda i: (i, 0))
    kernel = functools.partial(_outproj_kernel, prompt_tiles=prompt_tiles, alpha=alpha, widths=widths)
    return pl.pallas_call(
        kernel,
        out_shape=(jax.ShapeDtypeStruct((n, d), F32), jax.ShapeDtypeStruct((n, d), BF16),
                   jax.ShapeDtypeStruct((n, LANES), jnp.int32), jax.ShapeDtypeStruct((n, LANES), F32)),
        grid=(n // ROW_TILE,),
        in_specs=[row_d] + [p_spec(wd) for wd in widths] + [s_spec(wd) for wd in widths]
                 + [full((d, d)), full((1, d)), full((1, d)), full((d, N_EXPERTS)), full((1, N_EXPERTS))],
        out_specs=(row_d, row_d, row_l, row_l),
        name="out_proj_ln_router",
        compiler_params=_params("parallel"),
    )(x, *prompt_mix, *sample_mix, w_out_bf16, ln_g.reshape(1, d), ln_b.reshape(1, d),
      w_router, b_router.reshape(1, N_EXPERTS))


def _expert_kernel(be_ref, nv_ref, xs_ref, wgu_ref, bgu_ref, wd_ref, bd_ref, y_ref, *, d_ff):
    del be_ref
    valid = pl.program_id(0) < nv_ref[0]

    @pl.when(valid)
    def _():
        h = _dot(xs_ref[...], wgu_ref[...]) + bgu_ref[...]
        g = jnp.minimum(h[:, :d_ff], SWIGLU_LIMIT)
        u = jnp.clip(h[:, d_ff:], -SWIGLU_LIMIT, SWIGLU_LIMIT)
        hid = (u + 1.0) * g * _sigmoid(SWIGLU_ALPHA * g)
        y_ref[...] = _dot(hid.astype(BF16), wd_ref[...]) + bd_ref[...]

    @pl.when(jnp.logical_not(valid))
    def _():
        y_ref[...] = jnp.zeros_like(y_ref)


def expert_ffn(xs, block_expert, n_valid, w_gu, b_gu, w_down, b_down):
    n_rows, d = xs.shape
    n_blocks = n_rows // MOE_ROWS
    d_ff = w_down.shape[1]

    def in_block(i, be, nv):
        return jnp.minimum(i, nv[0] - 1)

    kernel = functools.partial(_expert_kernel, d_ff=d_ff)
    return pl.pallas_call(
        kernel,
        out_shape=jax.ShapeDtypeStruct((n_rows, d), F32),
        grid_spec=pltpu.PrefetchScalarGridSpec(
            num_scalar_prefetch=2,
            grid=(n_blocks,),
            in_specs=[pl.BlockSpec((MOE_ROWS, d), lambda i, be, nv: (in_block(i, be, nv), 0)),
                      pl.BlockSpec((None, d, 2 * d_ff), lambda i, be, nv: (be[i], 0, 0)),
                      pl.BlockSpec((None, 1, 2 * d_ff), lambda i, be, nv: (be[i], 0, 0)),
                      pl.BlockSpec((None, d_ff, d), lambda i, be, nv: (be[i], 0, 0)),
                      pl.BlockSpec((None, 1, d), lambda i, be, nv: (be[i], 0, 0))],
            out_specs=pl.BlockSpec((MOE_ROWS, d), lambda i, be, nv: (i, 0))),
        name="expert_ffn",
        compiler_params=_params("arbitrary"),
    )(block_expert, n_valid, xs, w_gu, b_gu[:, None, :], w_down, b_down[:, None, :])


def route(idx, n_tok):
    n_slots = n_tok * TOP_K
    e_flat = idx[:n_tok, :TOP_K].reshape(n_slots)
    tok_flat = jnp.repeat(jnp.arange(n_tok, dtype=jnp.int32), TOP_K)
    order = jnp.argsort(e_flat)
    e_sorted = e_flat[order]
    counts = jnp.bincount(e_flat, length=N_EXPERTS)
    padded = (counts + MOE_ROWS - 1) // MOE_ROWS * MOE_ROWS
    start = jnp.cumsum(counts) - counts
    end_pad = jnp.cumsum(padded)
    start_pad = end_pad - padded
    dest = (start_pad[e_sorted] + jnp.arange(n_slots, dtype=jnp.int32) - start[e_sorted]).astype(jnp.int32)
    n_blocks = -(-n_slots // MOE_ROWS) + N_EXPERTS
    n_rows = n_blocks * MOE_ROWS
    row_tok = jnp.full((n_rows,), n_tok, jnp.int32).at[dest].set(tok_flat[order])
    slot_row = jnp.zeros((n_slots,), jnp.int32).at[order].set(dest)
    block_expert = jnp.minimum(
        jnp.searchsorted(end_pad, jnp.arange(n_blocks) * MOE_ROWS, side='right'), N_EXPERTS - 1).astype(jnp.int32)
    n_valid = (end_pad[-1] // MOE_ROWS).astype(jnp.int32).reshape(1)
    return row_tok, slot_row.reshape(n_tok, TOP_K), block_expert, n_valid


def _combine_kernel(x_ref, y_ref, gate_ref, g_ref, b_ref, o_ref, *, alpha):
    gates = gate_ref[...]
    lane = lax.broadcasted_iota(jnp.int32, gates.shape, 1)
    moe = jnp.zeros(x_ref.shape, F32)
    for k in range(TOP_K):
        gk = jnp.sum(jnp.where(lane == k, gates, 0.0), axis=1, keepdims=True)
        moe = moe + gk * y_ref[k]
    o_ref[...] = _layer_norm(alpha * x_ref[...] + moe, g_ref[...], b_ref[...])


def combine_ln(x1, y_sel, gates, ln_g, ln_b, *, alpha):
    n, d = x1.shape
    kernel = functools.partial(_combine_kernel, alpha=alpha)
    return pl.pallas_call(
        kernel,
        out_shape=jax.ShapeDtypeStruct((n, d), F32),
        grid=(n // ROW_TILE,),
        in_specs=[pl.BlockSpec((ROW_TILE, d), lambda i: (i, 0)),
                  pl.BlockSpec((TOP_K, ROW_TILE, d), lambda i: (0, i, 0)),
                  pl.BlockSpec((ROW_TILE, LANES), lambda i: (i, 0)),
                  pl.BlockSpec((1, d), lambda i: (0, 0)),
                  pl.BlockSpec((1, d), lambda i: (0, 0))],
        out_specs=pl.BlockSpec((ROW_TILE, d), lambda i: (i, 0)),
        name="combine_ln",
        compiler_params=_params("parallel"),
    )(x1, y_sel, gates, ln_g.reshape(1, d), ln_b.reshape(1, d))


def kernel(x_prompt, x_sample, cache_k, cache_v, page_table, state_hgrn, state_lru, state_conv, ln_in_g, ln_in_b,
           w_in, sb_bias, hg_lb, hg_norm_g, lru_conv_w, lru_conv_b, lru_wa, lru_ba, lru_wx, lru_bx, lru_lam,
           w_out, ln1_g, ln1_b, w_router, b_router, w_gu, b_gu, w_down, b_down, ln2_g, ln2_b):
    bp, tp, d = x_prompt.shape
    bs, ts, _ = x_sample.shape
    assert ts == 1
    depth = w_in.shape[0]
    att_heads = cache_k.shape[3]
    att_w = att_heads * HEAD_DIM
    hg_heads = state_hgrn.shape[2]
    hg_w = hg_heads * HEAD_DIM
    lru_w = state_lru.shape[2]
    assert w_in.shape[2] == 3 * att_w + 4 * hg_w + 2 * lru_w
    col_q, col_k, col_v = 0, att_w, 2 * att_w
    col_hq = 3 * att_w
    col_hf, col_hi, col_hg = col_hq + hg_w, col_hq + 2 * hg_w, col_hq + 3 * hg_w
    col_lx = col_hq + 4 * hg_w
    col_lg = col_lx + lru_w
    alpha = (2 * depth) ** 0.25

    n_prompt = bp * tp
    n_tok = n_prompt + bs
    assert n_prompt % ROW_TILE == 0 and bs <= ROW_TILE
    n_all = n_prompt + ROW_TILE

    lb_cum = jnp.cumsum(jax.nn.softmax(hg_lb.astype(F32), axis=0), axis=0)
    lb_all = lb_cum - lb_cum[0]

    x_all = jnp.concatenate([x_prompt.reshape(n_prompt, d), x_sample.reshape(bs, d),
                             jnp.zeros((n_all - n_tok, d), F32)], axis=0)
    x = layer_norm_rows(x_all, ln_in_g, ln_in_b)

    n_pool, page = cache_k.shape[1], cache_k.shape[2]
    ck = cache_k.reshape(depth, n_pool, page, att_w)
    cv = cache_v.reshape(depth, n_pool, page, att_w)

    outs_p, outs_s = [], []
    for l in range(depth):
        proj = in_proj(x, w_in[l].astype(BF16))
        lru_weights = (lru_conv_w[l], lru_conv_b[l].reshape(1, lru_w),
                       _block_diag(lru_wa[l]).astype(BF16), lru_ba[l].reshape(1, lru_w),
                       _block_diag(lru_wx[l]).astype(BF16), lru_bx[l].reshape(1, lru_w),
                       lru_lam[l].reshape(1, lru_w))

        att_p = attn_prompt(proj, sb_bias[l], batch=bp, seq=tp, col_q=col_q, col_k=col_k, col_v=col_v,
                            n_heads=att_heads)
        hg_p, st_p = hgrn_prompt(proj, lb_all[l], hg_norm_g[l], batch=bp, seq=tp, col_q=col_hq, col_f=col_hf,
                                 col_i=col_hi, col_g=col_hg, n_heads=hg_heads)
        lru_p, hl_p = lru_prompt(proj, lru_weights, batch=bp, seq=tp, col_x=col_lx, col_g=col_lg, width=lru_w)

        proj_s = proj[n_prompt:n_tok]
        att_s = attn_decode(proj_s[:, col_q:col_q + att_w], sb_bias[l], ck, cv, page_table, l, out_rows=ROW_TILE)
        hg_s, st_s = hgrn_decode(proj_s[:, col_hq:col_hq + 4 * hg_w], state_hgrn[l], lb_all[l], hg_norm_g[l],
                                 out_rows=ROW_TILE)
        lx_s = proj_s[:, col_lx:col_lx + lru_w]
        lru_s, hl_s = lru_decode(lx_s, proj_s[:, col_lg:col_lg + lru_w], state_conv[l], state_lru[l], lru_weights,
                                 out_rows=ROW_TILE)

        x1, x1b, idx, gates = out_proj_ln_router(
            x, (att_p, hg_p, lru_p), (att_s, hg_s, lru_s), w_out[l].astype(BF16), ln1_g[l], ln1_b[l],
            w_router[l], b_router[l], n_prompt=n_prompt, alpha=alpha)

        row_tok, slot_row, block_expert, n_valid = route(idx, n_tok)
        xs = x1b[row_tok]
        ys = expert_ffn(xs, block_expert, n_valid, w_gu[l].astype(BF16), b_gu[l], w_down[l].astype(BF16), b_down[l])
        slot_pad = jnp.concatenate([slot_row, jnp.zeros((n_all - n_tok, TOP_K), jnp.int32)], axis=0)
        y_sel = ys[slot_pad.T]
        x = combine_ln(x1, y_sel, gates, ln2_g[l], ln2_b[l], alpha=alpha)

        k_p = proj[:n_prompt, col_k:col_k + att_w].reshape(bp, tp, att_heads, HEAD_DIM)
        v_p = proj[:n_prompt, col_v:col_v + att_w].reshape(bp, tp, att_heads, HEAD_DIM)
        pairs = hg_heads // 2
        s_blocks = [st_p[:, hh // 2, (hh % 2) * HEAD_DIM:(hh % 2 + 1) * HEAD_DIM,
                         (hh % 2) * HEAD_DIM:(hh % 2 + 1) * HEAD_DIM] for hh in range(hg_heads)]
        del pairs
        s_p = jnp.swapaxes(jnp.stack(s_blocks, axis=1), -1, -2)
        lx_p = proj[:n_prompt, col_lx:col_lx + lru_w].reshape(bp, tp, lru_w)
        c_p = lx_p[:, tp - (CONV_W - 1):]
        outs_p.append((k_p, v_p, s_p, hl_p.reshape(bp, lru_w), c_p))

        k_s = proj_s[:, col_k:col_k + att_w].reshape(bs, ts, att_heads, HEAD_DIM)
        v_s = proj_s[:, col_v:col_v + att_w].reshape(bs, ts, att_heads, HEAD_DIM)
        c_s = jnp.concatenate([state_conv[l][:, 1:], lx_s[:, None, :]], axis=1)
        outs_s.append((k_s, v_s, st_s, hl_s, c_s))

    def stacked(rows, i):
        return jnp.stack([r[i] for r in rows])

    hp = x[:n_prompt].reshape(bp, tp, d)
    hs = x[n_prompt:n_tok].reshape(bs, ts, d)
    return (hp, hs,
            stacked(outs_p, 0), stacked(outs_p, 1), stacked(outs_p, 2), stacked(outs_p, 3), stacked(outs_p, 4),
            stacked(outs_s, 0), stacked(outs_s, 1), stacked(outs_s, 2), stacked(outs_s, 3), stacked(outs_s, 4))
```

```python
import functools

import jax
import jax.numpy as jnp
from jax import lax
from jax.experimental import pallas as pl
from jax.experimental.pallas import tpu as pltpu

F32 = jnp.float32
BF16 = jnp.bfloat16
HIGHEST = lax.Precision.HIGHEST

HEAD_DIM = 64
LANES = 128
CONV_W = 4
LRU_C = 8.0
N_EXPERTS = 32
TOP_K = 4
SWIGLU_LIMIT = 7.0
SWIGLU_ALPHA = 1.702
LN_EPS = 1e-5
RMS_EPS = 1e-6

ROW_TILE = 256
ATT_TILE = 256
HG_CHUNK = 64
HG_SUB = 16
HG_STEP_ROWS = 512
LRU_STEP_ROWS = 512
MOE_ROWS = 256
VMEM_LIMIT = 48 * 2**20
DECODE_VMEM_LIMIT = 54 * 2**20
LOG2E = 1.4426950408889634

NT_DIMS = (((1,), (1,)), ((), ()))
TN_DIMS = (((0,), (0,)), ((), ()))


def _params(*sem):
    return pltpu.CompilerParams(dimension_semantics=sem, vmem_limit_bytes=VMEM_LIMIT)


def _dot(a, b, precision=None):
    return jnp.dot(a, b, preferred_element_type=F32, precision=precision)


def _dot_nt(a, b):
    return lax.dot_general(a, b, NT_DIMS, preferred_element_type=F32)


def _sigmoid(x):
    return 1.0 / (1.0 + jnp.exp(-x))


def _log_sigmoid(x):
    return jnp.minimum(x, 0.0) - jnp.log1p(jnp.exp(-jnp.abs(x)))


def _layer_norm(x, g, b):
    mu = jnp.mean(x, axis=-1, keepdims=True)
    xc = x - mu
    var = jnp.mean(xc * xc, axis=-1, keepdims=True)
    return xc * lax.rsqrt(var + LN_EPS) * g + b


def _gelu_tanh(x):
    return 0.5 * x * (1.0 + jnp.tanh(0.7978845608028654 * (x + 0.044715 * (x * x * x))))


def _ln_kernel(x_ref, g_ref, b_ref, o_ref):
    o_ref[...] = _layer_norm(x_ref[...], g_ref[...], b_ref[...])


def layer_norm_rows(x, g, b):
    n, d = x.shape
    return pl.pallas_call(
        _ln_kernel,
        out_shape=jax.ShapeDtypeStruct((n, d), F32),
        grid=(n // ROW_TILE,),
        in_specs=[pl.BlockSpec((ROW_TILE, d), lambda i: (i, 0)),
                  pl.BlockSpec((1, d), lambda i: (0, 0)),
                  pl.BlockSpec((1, d), lambda i: (0, 0))],
        out_specs=pl.BlockSpec((ROW_TILE, d), lambda i: (i, 0)),
        name="ln_in",
        compiler_params=_params("parallel"),
    )(x, g.reshape(1, d), b.reshape(1, d))


def _inproj_kernel(x_ref, w_ref, o_ref):
    o_ref[...] = _dot(x_ref[...].astype(BF16), w_ref[...])


def in_proj(x, w_bf16):
    n, d = x.shape
    cols = w_bf16.shape[1]
    return pl.pallas_call(
        _inproj_kernel,
        out_shape=jax.ShapeDtypeStruct((n, cols), F32),
        grid=(n // ROW_TILE,),
        in_specs=[pl.BlockSpec((ROW_TILE, d), lambda i: (i, 0)),
                  pl.BlockSpec((d, cols), lambda i: (0, 0))],
        out_specs=pl.BlockSpec((ROW_TILE, cols), lambda i: (i, 0)),
        name="in_proj",
        compiler_params=_params("parallel"),
    )(x, w_bf16)


def _softplus2(z):
    return jnp.maximum(z, 0.0) + jnp.log(1.0 + jnp.exp2(-jnp.abs(z))) * LOG2E


def _attn_prompt_kernel(bias_ref, q_ref, k_ref, v_ref, o_ref, kb_ref, vb_ref, acc_ref, c_ref, *, tile):
    hp = pl.program_id(1)
    i = pl.program_id(2)

    @pl.when(i == 0)
    def _():
        kb_ref[...] = k_ref[...].astype(BF16)
        vb_ref[...] = v_ref[...].astype(BF16)

    lane = lax.broadcasted_iota(jnp.int32, (1, LANES), 1)
    head_lanes = (lane < HEAD_DIM, lane >= HEAD_DIM)
    q = q_ref[...] * (HEAD_DIM ** -0.5 * LOG2E)
    qh = [jnp.where(m, q, 0.0).astype(BF16) for m in head_lanes]
    bias = [bias_ref[pl.ds(2 * hp + a, 1), :] for a in range(2)]
    row = lax.broadcasted_iota(jnp.int32, (tile, tile), 0)
    col = lax.broadcasted_iota(jnp.int32, (tile, tile), 1)
    neg_later = jnp.where(row > col, -1.0, 0.0).astype(BF16)
    causal = col < row
    acc_ref[...] = jnp.zeros_like(acc_ref)
    c_ref[...] = jnp.zeros_like(c_ref)

    def key_tile(j, diag):
        start = pl.multiple_of(j * tile, tile)
        kb = kb_ref[pl.ds(start, tile), :]
        vb = vb_ref[pl.ds(start, tile), :]
        for a in range(2):
            z = _dot_nt(qh[a], kb) + bias[a]
            sp = _softplus2(z)
            log_break = z - sp
            if diag:
                sp = jnp.where(causal, sp, 0.0)
            after = _dot(sp.astype(BF16), neg_later) + c_ref[a]
            w = jnp.exp2(log_break + after)
            if diag:
                w = jnp.where(causal, w, 0.0)
            acc_ref[a] += _dot(w.astype(BF16), vb)
            c_ref[a] -= jnp.sum(sp, axis=1, keepdims=True)

    key_tile(i, True)

    def pair(it, carry):
        key_tile(i - 1 - 2 * it, False)
        key_tile(i - 2 - 2 * it, False)
        return carry

    lax.fori_loop(0, i // 2, pair, 0)

    @pl.when(i % 2 == 1)
    def _():
        key_tile(0, False)

    o_ref[...] = jnp.where(head_lanes[0], acc_ref[0], acc_ref[1]).astype(o_ref.dtype)


def attn_prompt(proj, sb_bias, *, batch, seq, col_q, col_k, col_v, n_heads):
    tile = min(ATT_TILE, seq)
    assert seq % tile == 0
    nq = seq // tile
    pairs = n_heads // 2
    bias_b = jnp.broadcast_to((sb_bias.astype(F32) * LOG2E)[:, None], (n_heads, tile))
    kernel = functools.partial(_attn_prompt_kernel, tile=tile)
    return pl.pallas_call(
        kernel,
        out_shape=jax.ShapeDtypeStruct((batch * seq, n_heads * HEAD_DIM), BF16),
        grid=(batch, pairs, nq),
        in_specs=[pl.BlockSpec((n_heads, tile), lambda b, hp, i: (0, 0)),
                  pl.BlockSpec((tile, LANES), lambda b, hp, i: (b * nq + i, col_q // LANES + hp)),
                  pl.BlockSpec((seq, LANES), lambda b, hp, i: (b, col_k // LANES + hp)),
                  pl.BlockSpec((seq, LANES), lambda b, hp, i: (b, col_v // LANES + hp))],
        out_specs=pl.BlockSpec((tile, LANES), lambda b, hp, i: (b * nq + i, hp)),
        scratch_shapes=[pltpu.VMEM((seq, LANES), BF16), pltpu.VMEM((seq, LANES), BF16),
                        pltpu.VMEM((2, tile, LANES), F32), pltpu.VMEM((2, tile, 1), F32)],
        name="attn_prompt",
        compiler_params=_params("parallel", "parallel", "arbitrary"),
    )(bias_b, proj, proj, proj)


def _attn_decode_kernel(pt_ref, q_ref, bias_ref, *refs, n_pages, page, n_heads):
    del pt_ref
    k_refs = refs[:n_pages]
    v_refs = refs[n_pages:2 * n_pages]
    o_ref = refs[2 * n_pages]
    z_ref = refs[2 * n_pages + 1]
    b = pl.program_id(0)
    span = page * n_heads

    @pl.when(b == 0)
    def _():
        o_ref[...] = jnp.zeros_like(o_ref)

    own = (lax.broadcasted_iota(jnp.int32, (n_heads, span), 1) % n_heads
           == lax.broadcasted_iota(jnp.int32, (n_heads, span), 0))
    qb = (q_ref[b] * (HEAD_DIM ** -0.5 * LOG2E)).astype(BF16)
    for p in range(n_pages):
        kp = k_refs[p][...].reshape(span, HEAD_DIM).astype(BF16)
        g = _dot_nt(qb, kp)
        z_ref[p:p + 1, :] = jnp.sum(jnp.where(own, g, 0.0), axis=0, keepdims=True)
    z = z_ref[...] + bias_ref[...]
    sp = _softplus2(z)
    lane = lax.broadcasted_iota(jnp.int32, (n_pages, span), 1)
    incl = sp
    total = sp
    shift = n_heads
    while shift < span:
        incl = incl + jnp.where(lane + shift < span, pltpu.roll(incl, span - shift, 1), 0.0)
        total = total + pltpu.roll(total, shift, 1)
        shift *= 2
    pr = lax.broadcasted_iota(jnp.int32, (n_pages, n_pages), 0)
    pc = lax.broadcasted_iota(jnp.int32, (n_pages, n_pages), 1)
    later_pages = _dot(jnp.where(pc > pr, 1.0, 0.0), total, HIGHEST)
    w = jnp.exp2(z - sp - (incl - sp) - later_pages)
    acc = jnp.zeros((n_heads, HEAD_DIM), F32)
    for p in range(n_pages):
        wm = jnp.where(own, w[p:p + 1, :], 0.0).astype(BF16)
        acc += _dot(wm, v_refs[p][...].reshape(span, HEAD_DIM).astype(BF16))
    o_ref[b] = acc


def attn_decode(q_heads, sb_bias, cache_k, cache_v, page_table, layer, *, out_rows):
    n_seq, n_heads, _ = q_heads.shape
    n_pages = page_table.shape[1]
    page = cache_k.shape[2]
    span = page * n_heads
    assert span & (span - 1) == 0 and n_heads & (n_heads - 1) == 0
    bias_lane = jnp.tile(sb_bias.astype(F32) * LOG2E, page)[None, :]

    def page_spec(p):
        return pl.BlockSpec((None, None, page, n_heads, HEAD_DIM), lambda b, pt: (layer, pt[b, p], 0, 0, 0))

    kernel = functools.partial(_attn_decode_kernel, n_pages=n_pages, page=page, n_heads=n_heads)
    return pl.pallas_call(
        kernel,
        out_shape=jax.ShapeDtypeStruct((out_rows, n_heads, HEAD_DIM), F32),
        grid_spec=pltpu.PrefetchScalarGridSpec(
            num_scalar_prefetch=1,
            grid=(n_seq,),
            in_specs=([pl.BlockSpec((n_seq, n_heads, HEAD_DIM), lambda b, pt: (0, 0, 0)),
                       pl.BlockSpec((1, span), lambda b, pt: (0, 0))]
                      + [page_spec(p) for p in range(n_pages)]
                      + [page_spec(p) for p in range(n_pages)]),
            out_specs=pl.BlockSpec((out_rows, n_heads, HEAD_DIM), lambda b, pt: (0, 0, 0)),
            scratch_shapes=[pltpu.VMEM((n_pages, span), F32)]),
        name="attn_decode",
        compiler_params=pltpu.CompilerParams(dimension_semantics=("arbitrary",), vmem_limit_bytes=DECODE_VMEM_LIMIT),
    )(page_table, q_heads, bias_lane, *([cache_k] * n_pages), *([cache_v] * n_pages))


def _hgrn_gates(hf, lb):
    a1 = jnp.log(lb)
    a2 = jnp.log1p(-lb) + jnp.minimum(hf, 0.0) - jnp.log(1.0 + jnp.exp(-jnp.abs(hf)))
    log_f = jnp.maximum(a1, a2) + jnp.log(1.0 + jnp.exp(-jnp.abs(a1 - a2)))
    key = (1.0 - lb) * (1.0 / (1.0 + jnp.exp(hf)))
    return log_f, key


def _head_block_mask(n):
    r = lax.broadcasted_iota(jnp.int32, (n, n), 0) // HEAD_DIM
    c = lax.broadcasted_iota(jnp.int32, (n, n), 1) // HEAD_DIM
    return r == c


def _hgrn_finish(o, gate_pre, norm_g, same_head):
    ms = _dot(o * o, jnp.where(same_head, 1.0 / HEAD_DIM, 0.0), HIGHEST)
    return o * lax.rsqrt(ms + RMS_EPS) * norm_g * (gate_pre * _sigmoid(gate_pre))


def _hgrn_chunk(q, hf, v, gate_pre, lb, norm_g, st):
    C, S = HG_CHUNK, HG_SUB
    lane = lax.broadcasted_iota(jnp.int32, (1, LANES), 1)
    head_lanes = (lane < HEAD_DIM, lane >= HEAD_DIM)
    same_head = _head_block_mask(LANES)
    ones_head = jnp.where(same_head, 1.0, 0.0).astype(BF16)
    tri = jnp.where(lax.broadcasted_iota(jnp.int32, (C, C), 0) >= lax.broadcasted_iota(jnp.int32, (C, C), 1),
                    1.0, 0.0)
    key_pos = lax.broadcasted_iota(jnp.int32, (S, C), 1)
    sub_row = lax.broadcasted_iota(jnp.int32, (S, 1), 0)

    log_f, k = _hgrn_gates(hf, lb)
    b = _dot(tri, log_f, HIGHEST)
    vb = v.astype(BF16)
    o = _dot_nt((q * jnp.exp(b)).astype(BF16), st.astype(BF16))

    att = [[jnp.zeros((S, C), F32)], [jnp.zeros((S, C), F32)]]
    for s0 in range(S, C, S):
        ref = b[s0 - 1:s0, :]
        qt = q[s0:s0 + S, :] * jnp.exp(b[s0:s0 + S, :] - ref)
        kt = (k * jnp.exp(jnp.minimum(ref - b, 0.0))).astype(BF16)
        for a in range(2):
            blk = _dot_nt(jnp.where(head_lanes[a], qt, 0.0).astype(BF16), kt)
            att[a].append(jnp.where(key_pos < s0, blk, 0.0))
    far = [_dot(jnp.concatenate(att[a], axis=0).astype(BF16), vb) for a in range(2)]
    o = o + jnp.where(head_lanes[0], far[0], far[1])

    near = []
    for s0 in range(0, C, S):
        qs = q[s0:s0 + S, :]
        bs = b[s0:s0 + S, :]
        terms = []
        for sl in range(S):
            s = s0 + sl
            decay = jnp.exp(jnp.where(sub_row >= sl, bs - b[s:s + 1, :], -jnp.inf))
            terms.append((qs * k[s:s + 1, :] * decay).astype(BF16))
        wgt = _dot(jnp.concatenate(terms, axis=0), ones_head)
        acc = jnp.zeros((S, LANES), F32)
        for sl in range(S):
            acc = acc + wgt[sl * S:(sl + 1) * S, :] * v[s0 + sl:s0 + sl + 1, :]
        near.append(acc)
    o = o + jnp.concatenate(near, axis=0)

    b_last = b[C - 1:C, :]
    upd = lax.dot_general(vb, (k * jnp.exp(b_last - b)).astype(BF16), TN_DIMS, preferred_element_type=F32)
    st_new = st * jnp.exp(b_last) + jnp.where(same_head, upd, 0.0)
    return _hgrn_finish(o, gate_pre, norm_g, same_head), st_new


def _hgrn_prompt_kernel(lb_ref, g_ref, hq_ref, hf_ref, hi_ref, hg_ref, o_ref, st_out_ref, st_ref, *, chunks, pairs):
    @pl.when(pl.program_id(1) == 0)
    def _():
        st_ref[...] = jnp.zeros_like(st_ref)

    def chunk(ci, carry):
        rows = pl.ds(pl.multiple_of(ci * HG_CHUNK, HG_CHUNK), HG_CHUNK)
        for p in range(pairs):
            cols = slice(p * LANES, (p + 1) * LANES)
            out, st_new = _hgrn_chunk(hq_ref[rows, cols], hf_ref[rows, cols], hi_ref[rows, cols], hg_ref[rows, cols],
                                      lb_ref[:, cols], g_ref[:, cols], st_ref[p])
            st_ref[p] = st_new
            o_ref[rows, cols] = out.astype(o_ref.dtype)
        return carry

    lax.fori_loop(0, chunks, chunk, 0)
    st_out_ref[...] = st_ref[...]


def hgrn_prompt(proj, lb, norm_g, *, batch, seq, col_q, col_f, col_i, col_g, n_heads):
    step = min(HG_STEP_ROWS, seq)
    assert seq % step == 0 and step % HG_CHUNK == 0
    ns = seq // step
    pairs = n_heads // 2
    width = n_heads * HEAD_DIM
    assert all(c % width == 0 for c in (col_q, col_f, col_i, col_g))

    def col_spec(col):
        return pl.BlockSpec((step, width), lambda b, i: (b * ns + i, col // width))

    kernel = functools.partial(_hgrn_prompt_kernel, chunks=step // HG_CHUNK, pairs=pairs)
    return pl.pallas_call(
        kernel,
        out_shape=(jax.ShapeDtypeStruct((batch * seq, width), BF16),
                   jax.ShapeDtypeStruct((batch, pairs, LANES, LANES), F32)),
        grid=(batch, ns),
        in_specs=[pl.BlockSpec((1, width), lambda b, i: (0, 0)),
                  pl.BlockSpec((1, width), lambda b, i: (0, 0)),
                  col_spec(col_q), col_spec(col_f), col_spec(col_i), col_spec(col_g)],
        out_specs=(pl.BlockSpec((step, width), lambda b, i: (b * ns + i, 0)),
                   pl.BlockSpec((None, pairs, LANES, LANES), lambda b, i: (b, 0, 0, 0))),
        scratch_shapes=[pltpu.VMEM((pairs, LANES, LANES), F32)],
        name="hgrn_prompt",
        compiler_params=_params("parallel", "arbitrary"),
    )(lb.reshape(1, width), norm_g.reshape(1, width), proj, proj, proj, proj)


def _hgrn_decode_kernel(lb_ref, g_ref, p_ref, s_ref, o_ref, s_out_ref, pt_ref, *, n_seq, n_heads):
    h = pl.program_id(0)
    width = n_heads * HEAD_DIM
    D = HEAD_DIM

    @pl.when(h == 0)
    def _():
        o_ref[...] = jnp.zeros_like(o_ref)
        pr = p_ref[...]
        log_f, k = _hgrn_gates(pr[:, width:2 * width], lb_ref[...])
        pt_ref[0] = pr[:, 0:width].T
        pt_ref[1] = jnp.exp(log_f).T
        pt_ref[2] = k.T
        pt_ref[3] = pr[:, 2 * width:3 * width].T

    rows = pl.ds(pl.multiple_of(h * D, D), D)
    q = pt_ref[0, rows, :]
    f = pt_ref[1, rows, :]
    k = pt_ref[2, rows, :]
    v = pt_ref[3, rows, :]
    st = s_ref[...].T.reshape(D, D, n_seq)
    new = st * f[:, None, :] + k[:, None, :] * v[None, :, :]
    s_out_ref[...] = new.reshape(D * D, n_seq).T
    o_t = jnp.sum(new * q[:, None, :], axis=0)
    pt_ref[4, rows, :] = o_t

    @pl.when(h == n_heads - 1)
    def _():
        o = pt_ref[4].T
        pr = p_ref[...]
        same_head = _head_block_mask(width)
        out = _hgrn_finish(o, pr[:, 3 * width:4 * width], g_ref[...], same_head)
        o_ref[0:n_seq, :] = out.astype(o_ref.dtype)


def hgrn_decode(proj_h, state, lb, norm_g, *, out_rows):
    n_seq, n_heads = state.shape[0], state.shape[1]
    width = n_heads * HEAD_DIM
    dd = HEAD_DIM * HEAD_DIM
    kernel = functools.partial(_hgrn_decode_kernel, n_seq=n_seq, n_heads=n_heads)
    out, new_state = pl.pallas_call(
        kernel,
        out_shape=(jax.ShapeDtypeStruct((out_rows, width), BF16),
                   jax.ShapeDtypeStruct((n_seq, n_heads * dd), F32)),
        grid=(n_heads,),
        in_specs=[pl.BlockSpec((1, width), lambda h: (0, 0)),
                  pl.BlockSpec((1, width), lambda h: (0, 0)),
                  pl.BlockSpec((n_seq, 4 * width), lambda h: (0, 0)),
                  pl.BlockSpec((n_seq, dd), lambda h: (0, h))],
        out_specs=(pl.BlockSpec((out_rows, width), lambda h: (0, 0)),
                   pl.BlockSpec((n_seq, dd), lambda h: (0, h))),
        scratch_shapes=[pltpu.VMEM((5, width, n_seq), F32)],
        name="hgrn_decode",
        compiler_params=_params("arbitrary"),
    )(lb.reshape(1, width), norm_g.reshape(1, width), proj_h, state.reshape(n_seq, n_heads * dd))
    return out, new_state.reshape(state.shape)


def _lru_gates(conv, wa_ref, ba_ref, wx_ref, bx_ref, lam_ref):
    cb = conv.astype(BF16)
    r = _sigmoid(_dot(cb, wa_ref[...]) + ba_ref[...])
    i = _sigmoid(_dot(cb, wx_ref[...]) + bx_ref[...])
    log_a = LRU_C * r * _log_sigmoid(lam_ref[...])
    th = jnp.tanh(log_a)
    mult = jnp.sqrt(-2.0 * th / (1.0 - th))
    return jnp.exp(log_a), conv * i, mult


def _lru_prompt_kernel(x_ref, g_ref, cw_ref, cb_ref, wa_ref, ba_ref, wx_ref, bx_ref, lam_ref,
                       y_ref, hl_ref, xbuf, a_scr, u_scr, h_scr, hc_scr, *, step):
    t = pl.program_id(1)

    @pl.when(t == 0)
    def _():
        xbuf[0:8, :] = jnp.zeros((8, xbuf.shape[1]), F32)
        hc_scr[...] = jnp.zeros_like(hc_scr)

    @pl.when(t > 0)
    def _():
        xbuf[0:8, :] = xbuf[step:step + 8, :]

    xbuf[8:step + 8, :] = x_ref[...]
    conv = cb_ref[...]
    for j in range(CONV_W):
        off = 8 - (CONV_W - 1) + j
        conv = conv + xbuf[off:off + step, :] * cw_ref[j:j + 1, :]
    a, ci, mult = _lru_gates(conv, wa_ref, ba_ref, wx_ref, bx_ref, lam_ref)
    first = (t * step + lax.broadcasted_iota(jnp.int32, (step, 1), 0)) == 0
    a_scr[...] = jnp.where(first, 0.0, a)
    u_scr[...] = ci * jnp.where(first, 1.0, mult)
    rid = lax.broadcasted_iota(jnp.int32, (8, 1), 0)

    def group(gi, hc):
        rows = pl.ds(pl.multiple_of(gi * 8, 8), 8)
        a8 = a_scr[rows, :]
        u8 = u_scr[rows, :]
        for sh in (1, 2, 4):
            ok = rid >= sh
            u8 = jnp.where(ok, a8 * pltpu.roll(u8, sh, 0) + u8, u8)
            a8 = jnp.where(ok, a8 * pltpu.roll(a8, sh, 0), a8)
        h8 = a8 * hc + u8
        h_scr[rows, :] = h8
        return h8[7:8, :]

    hc = lax.fori_loop(0, step // 8, group, hc_scr[...])
    hc_scr[...] = hc
    hl_ref[...] = hc
    y_ref[...] = (_gelu_tanh(g_ref[...]) * h_scr[...]).astype(y_ref.dtype)


def _block_diag(w):
    nb, bw, _ = w.shape
    out = jnp.zeros((nb * bw, nb * bw), w.dtype)
    for n in range(nb):
        out = out.at[n * bw:(n + 1) * bw, n * bw:(n + 1) * bw].set(w[n])
    return out


def lru_prompt(proj, weights, *, batch, seq, col_x, col_g, width):
    step = min(LRU_STEP_ROWS, seq)
    assert seq % step == 0 and step % 8 == 0 and seq >= CONV_W - 1
    ns = seq // step
    cw, cb, wa, ba, wx, bx, lam = weights
    vec = pl.BlockSpec((1, width), lambda b, i: (0, 0))
    mat = pl.BlockSpec((width, width), lambda b, i: (0, 0))
    kernel = functools.partial(_lru_prompt_kernel, step=step)
    return pl.pallas_call(
        kernel,
        out_shape=(jax.ShapeDtypeStruct((batch * seq, width), BF16),
                   jax.ShapeDtypeStruct((batch, 1, width), F32)),
        grid=(batch, ns),
        in_specs=[pl.BlockSpec((step, width), lambda b, i: (b * ns + i, col_x // width)),
                  pl.BlockSpec((step, width), lambda b, i: (b * ns + i, col_g // width)),
                  pl.BlockSpec((CONV_W, width), lambda b, i: (0, 0)), vec, mat, vec, mat, vec, vec],
        out_specs=(pl.BlockSpec((step, width), lambda b, i: (b * ns + i, 0)),
                   pl.BlockSpec((None, 1, width), lambda b, i: (b, 0, 0))),
        scratch_shapes=[pltpu.VMEM((step + 8, width), F32), pltpu.VMEM((step, width), F32),
                        pltpu.VMEM((step, width), F32), pltpu.VMEM((step, width), F32),
                        pltpu.VMEM((1, width), F32)],
        name="lru_prompt",
        compiler_params=_params("parallel", "arbitrary"),
    )(proj, proj, cw, cb, wa, ba, wx, bx, lam)


def _lru_decode_kernel(x_ref, g_ref, b0_ref, b1_ref, b2_ref, h0_ref, cw_ref, cb_ref, wa_ref, ba_ref, wx_ref,
                       bx_ref, lam_ref, y_ref, h_ref, *, n_seq):
    conv = (cb_ref[...] + b0_ref[...] * cw_ref[0:1, :] + b1_ref[...] * cw_ref[1:2, :]
            + b2_ref[...] * cw_ref[2:3, :] + x_ref[...] * cw_ref[3:4, :])
    a, ci, mult = _lru_gates(conv, wa_ref, ba_ref, wx_ref, bx_ref, lam_ref)
    h = a * h0_ref[...] + ci * mult
    h_ref[...] = h
    y_ref[...] = jnp.zeros_like(y_ref)
    y_ref[0:n_seq, :] = (_gelu_tanh(g_ref[...]) * h).astype(y_ref.dtype)


def lru_decode(x, g, conv_buf, h0, weights, *, out_rows):
    n_seq, width = x.shape
    cw, cb, wa, ba, wx, bx, lam = weights
    kernel = functools.partial(_lru_decode_kernel, n_seq=n_seq)
    return pl.pallas_call(
        kernel,
        out_shape=(jax.ShapeDtypeStruct((out_rows, width), BF16),
                   jax.ShapeDtypeStruct((n_seq, width), F32)),
        name="lru_decode",
    )(x, g, conv_buf[:, 0], conv_buf[:, 1], conv_buf[:, 2], h0, cw, cb, wa, ba, wx, bx, lam)


def _outproj_kernel(x_ref, att_p, hg_p, lru_p, att_s, hg_s, lru_s, w_ref, g_ref, b_ref, wr_ref, br_ref,
                    x1_ref, x1b_ref, idx_ref, gate_ref, *, prompt_tiles, alpha, widths):
    is_sample = pl.program_id(0) >= prompt_tiles
    wa, wh, wl = widths
    att = jnp.where(is_sample, att_s[...].astype(BF16), att_p[...])
    hg = jnp.where(is_sample, hg_s[...], hg_p[...])
    lru = jnp.where(is_sample, lru_s[...], lru_p[...])
    mix = (_dot(att, w_ref[0:wa, :]) + _dot(hg, w_ref[wa:wa + wh, :]) + _dot(lru, w_ref[wa + wh:wa + wh + wl, :]))
    x1 = _layer_norm(alpha * x_ref[...] + mix, g_ref[...], b_ref[...])
    x1_ref[...] = x1
    x1b_ref[...] = x1.astype(BF16)

    logits = _dot(x1, wr_ref[...], HIGHEST) + br_ref[...]
    rows, n_exp = logits.shape
    e_id = lax.broadcasted_iota(jnp.int32, (rows, n_exp), 1).astype(F32)
    out_lane = lax.broadcasted_iota(jnp.int32, (rows, LANES), 1)
    work = logits
    vals, idx_out = [], jnp.zeros((rows, LANES), F32)
    for k in range(TOP_K):
        m = jnp.max(work, axis=1, keepdims=True)
        ix = jnp.min(jnp.where(work == m, e_id, float(n_exp)), axis=1, keepdims=True)
        vals.append(m)
        idx_out = jnp.where(out_lane == k, ix, idx_out)
        work = jnp.where(e_id == ix, -jnp.inf, work)
    ex = [jnp.exp(v - vals[0]) for v in vals]
    inv = 1.0 / (ex[0] + ex[1] + ex[2] + ex[3])
    gate_out = jnp.zeros((rows, LANES), F32)
    for k in range(TOP_K):
        gate_out = jnp.where(out_lane == k, ex[k] * inv, gate_out)
    idx_ref[...] = idx_out.astype(jnp.int32)
    gate_ref[...] = gate_out


def out_proj_ln_router(x, prompt_mix, sample_mix, w_out_bf16, ln_g, ln_b, w_router, b_router, *, n_prompt, alpha):
    n, d = x.shape
    prompt_tiles = n_prompt // ROW_TILE
    widths = tuple(a.shape[1] for a in prompt_mix)
    last = prompt_tiles - 1

    def p_spec(wd):
        return pl.BlockSpec((ROW_TILE, wd), lambda i: (jnp.minimum(i, last), 0))

    def s_spec(wd):
        return pl.BlockSpec((ROW_TILE, wd), lambda i: (0, 0))

    def full(shape):
        return pl.BlockSpec(shape, lambda i: (0, 0))

    row_d = pl.BlockSpec((ROW_TILE, d), lambda i: (i, 0))
    row_l = pl.BlockSpec((ROW_TILE, LANES), lambda i: (i, 0))
    kernel = functools.partial(_outproj_kernel, prompt_tiles=prompt_tiles, alpha=alpha, widths=widths)
    return pl.pallas_call(
        kernel,
        out_shape=(jax.ShapeDtypeStruct((n, d), F32), jax.ShapeDtypeStruct((n, d), BF16),
                   jax.ShapeDtypeStruct((n, LANES), jnp.int32), jax.ShapeDtypeStruct((n, LANES), F32)),
        grid=(n // ROW_TILE,),
        in_specs=[row_d] + [p_spec(wd) for wd in widths] + [s_spec(wd) for wd in widths]
                 + [full((d, d)), full((1, d)), full((1, d)), full((d, N_EXPERTS)), full((1, N_EXPERTS))],
        out_specs=(row_d, row_d, row_l, row_l),
        name="out_proj_ln_router",
        compiler_params=_params("parallel"),
    )(x, *prompt_mix, *sample_mix, w_out_bf16, ln_g.reshape(1, d), ln_b.reshape(1, d),
      w_router, b_router.reshape(1, N_EXPERTS))


def _expert_kernel(be_ref, nv_ref, xs_ref, wgu_ref, bgu_ref, wd_ref, bd_ref, y_ref, *, d_ff):
    del be_ref
    valid = pl.program_id(0) < nv_ref[0]

    @pl.when(valid)
    def _():
        h = _dot(xs_ref[...], wgu_ref[...]) + bgu_ref[...]
        g = jnp.minimum(h[:, :d_ff], SWIGLU_LIMIT)
        u = jnp.clip(h[:, d_ff:], -SWIGLU_LIMIT, SWIGLU_LIMIT)
        hid = (u + 1.0) * g * _sigmoid(SWIGLU_ALPHA * g)
        y_ref[...] = (_dot(hid.astype(BF16), wd_ref[...]) + bd_ref[...]).astype(y_ref.dtype)

    @pl.when(jnp.logical_not(valid))
    def _():
        y_ref[...] = jnp.zeros_like(y_ref)


def expert_ffn(xs, block_expert, n_valid, w_gu, b_gu, w_down, b_down):
    n_rows, d = xs.shape
    n_blocks = n_rows // MOE_ROWS
    d_ff = w_down.shape[1]

    def in_block(i, be, nv):
        return jnp.minimum(i, nv[0] - 1)

    kernel = functools.partial(_expert_kernel, d_ff=d_ff)
    return pl.pallas_call(
        kernel,
        out_shape=jax.ShapeDtypeStruct((n_rows, d), BF16),
        grid_spec=pltpu.PrefetchScalarGridSpec(
            num_scalar_prefetch=2,
            grid=(n_blocks,),
            in_specs=[pl.BlockSpec((MOE_ROWS, d), lambda i, be, nv: (in_block(i, be, nv), 0)),
                      pl.BlockSpec((None, d, 2 * d_ff), lambda i, be, nv: (be[i], 0, 0)),
                      pl.BlockSpec((None, 1, 2 * d_ff), lambda i, be, nv: (be[i], 0, 0)),
                      pl.BlockSpec((None, d_ff, d), lambda i, be, nv: (be[i], 0, 0)),
                      pl.BlockSpec((None, 1, d), lambda i, be, nv: (be[i], 0, 0))],
            out_specs=pl.BlockSpec((MOE_ROWS, d), lambda i, be, nv: (i, 0))),
        name="expert_ffn",
        compiler_params=_params("arbitrary"),
    )(block_expert, n_valid, xs, w_gu, b_gu[:, None, :], w_down, b_down[:, None, :])


def route(idx, n_tok):
    n_slots = n_tok * TOP_K
    group = LANES
    assert n_slots % group == 0
    e_flat = idx[:n_tok, :TOP_K].reshape(n_slots)
    onehot = e_flat[:, None] == jnp.arange(N_EXPERTS, dtype=jnp.int32)[None, :]
    oh = onehot.astype(F32).reshape(n_slots // group, group, N_EXPERTS)
    tri = (jnp.arange(group)[:, None] >= jnp.arange(group)[None, :]).astype(F32)
    local = jnp.einsum('ts,gse->gte', tri, oh)
    g_total = local[:, -1, :]
    g_start = jnp.cumsum(g_total, axis=0) - g_total
    rank = jnp.sum(oh * (local + g_start[:, None, :] - 1.0), axis=-1).reshape(n_slots).astype(jnp.int32)
    counts = (g_start[-1] + g_total[-1]).astype(jnp.int32)
    padded = (counts + MOE_ROWS - 1) // MOE_ROWS * MOE_ROWS
    start = jnp.cumsum(counts) - counts
    end_pad = jnp.cumsum(padded)
    start_pad = end_pad - padded
    slot_row = jnp.sum(jnp.where(onehot, start_pad[None, :], 0), axis=-1) + rank
    n_blocks = -(-n_slots // MOE_ROWS) + N_EXPERTS
    n_rows = n_blocks * MOE_ROWS
    block_first = jnp.arange(n_blocks, dtype=jnp.int32) * MOE_ROWS
    block_expert = jnp.minimum(jnp.sum((end_pad[None, :] <= block_first[:, None]).astype(jnp.int32), axis=1),
                               N_EXPERTS - 1)
    n_valid = (end_pad[-1] // MOE_ROWS).astype(jnp.int32).reshape(1)
    key_sorted = jnp.sort(e_flat * n_slots + jnp.arange(n_slots, dtype=jnp.int32))
    tok_sorted = (key_sorted % n_slots) // TOP_K
    offset = jnp.arange(n_rows, dtype=jnp.int32) - jnp.repeat(start_pad[block_expert], MOE_ROWS)
    valid = offset < jnp.repeat(counts[block_expert], MOE_ROWS)
    source = jnp.clip(jnp.repeat(start[block_expert], MOE_ROWS) + offset, 0, n_slots - 1)
    row_tok = jnp.where(valid, tok_sorted[source], n_tok).astype(jnp.int32)
    return row_tok, slot_row.reshape(n_tok, TOP_K).astype(jnp.int32), block_expert.astype(jnp.int32), n_valid


def _combine_kernel(x_ref, y_ref, gate_ref, g_ref, b_ref, o_ref, *, alpha):
    gates = gate_ref[...]
    lane = lax.broadcasted_iota(jnp.int32, gates.shape, 1)
    moe = jnp.zeros(x_ref.shape, F32)
    for k in range(TOP_K):
        gk = jnp.sum(jnp.where(lane == k, gates, 0.0), axis=1, keepdims=True)
        moe = moe + gk * y_ref[k].astype(F32)
    o_ref[...] = _layer_norm(alpha * x_ref[...] + moe, g_ref[...], b_ref[...])


def combine_ln(x1, y_sel, gates, ln_g, ln_b, *, alpha):
    n, d = x1.shape
    kernel = functools.partial(_combine_kernel, alpha=alpha)
    return pl.pallas_call(
        kernel,
        out_shape=jax.ShapeDtypeStruct((n, d), F32),
        grid=(n // ROW_TILE,),
        in_specs=[pl.BlockSpec((ROW_TILE, d), lambda i: (i, 0)),
                  pl.BlockSpec((TOP_K, ROW_TILE, d), lambda i: (0, i, 0)),
                  pl.BlockSpec((ROW_TILE, LANES), lambda i: (i, 0)),
                  pl.BlockSpec((1, d), lambda i: (0, 0)),
                  pl.BlockSpec((1, d), lambda i: (0, 0))],
        out_specs=pl.BlockSpec((ROW_TILE, d), lambda i: (i, 0)),
        name="combine_ln",
        compiler_params=_params("parallel"),
    )(x1, y_sel, gates, ln_g.reshape(1, d), ln_b.reshape(1, d))


def kernel(x_prompt, x_sample, cache_k, cache_v, page_table, state_hgrn, state_lru, state_conv, ln_in_g, ln_in_b,
           w_in, sb_bias, hg_lb, hg_norm_g, lru_conv_w, lru_conv_b, lru_wa, lru_ba, lru_wx, lru_bx, lru_lam,
           w_out, ln1_g, ln1_b, w_router, b_router, w_gu, b_gu, w_down, b_down, ln2_g, ln2_b):
    bp, tp, d = x_prompt.shape
    bs, ts, _ = x_sample.shape
    assert ts == 1
    depth = w_in.shape[0]
    att_heads = cache_k.shape[3]
    att_w = att_heads * HEAD_DIM
    hg_heads = state_hgrn.shape[2]
    hg_w = hg_heads * HEAD_DIM
    lru_w = state_lru.shape[2]
    assert w_in.shape[2] == 3 * att_w + 4 * hg_w + 2 * lru_w
    col_q, col_k, col_v = 0, att_w, 2 * att_w
    col_hq = 3 * att_w
    col_hf, col_hi, col_hg = col_hq + hg_w, col_hq + 2 * hg_w, col_hq + 3 * hg_w
    col_lx = col_hq + 4 * hg_w
    col_lg = col_lx + lru_w
    alpha = (2 * depth) ** 0.25

    n_prompt = bp * tp
    n_tok = n_prompt + bs
    assert n_prompt % ROW_TILE == 0 and bs <= ROW_TILE
    n_all = n_prompt + ROW_TILE

    lb_cum = jnp.cumsum(jax.nn.softmax(hg_lb.astype(F32), axis=0), axis=0)
    lb_all = lb_cum - lb_cum[0]

    x_all = jnp.concatenate([x_prompt.reshape(n_prompt, d), x_sample.reshape(bs, d),
                             jnp.zeros((n_all - n_tok, d), F32)], axis=0)
    x = layer_norm_rows(x_all, ln_in_g, ln_in_b)

    outs_p, outs_s = [], []
    for l in range(depth):
        proj = in_proj(x, w_in[l].astype(BF16))
        lru_weights = (lru_conv_w[l], lru_conv_b[l].reshape(1, lru_w),
                       _block_diag(lru_wa[l]).astype(BF16), lru_ba[l].reshape(1, lru_w),
                       _block_diag(lru_wx[l]).astype(BF16), lru_bx[l].reshape(1, lru_w),
                       lru_lam[l].reshape(1, lru_w))

        att_p = attn_prompt(proj, sb_bias[l], batch=bp, seq=tp, col_q=col_q, col_k=col_k, col_v=col_v,
                            n_heads=att_heads)
        hg_p, st_p = hgrn_prompt(proj, lb_all[l], hg_norm_g[l], batch=bp, seq=tp, col_q=col_hq, col_f=col_hf,
                                 col_i=col_hi, col_g=col_hg, n_heads=hg_heads)
        lru_p, hl_p = lru_prompt(proj, lru_weights, batch=bp, seq=tp, col_x=col_lx, col_g=col_lg, width=lru_w)

        proj_s = proj[n_prompt:n_tok]
        q_s = proj_s[:, col_q:col_q + att_w].reshape(bs, att_heads, HEAD_DIM)
        att_s = attn_decode(q_s, sb_bias[l], cache_k, cache_v, page_table, l, out_rows=ROW_TILE)
        att_s = att_s.reshape(ROW_TILE, att_w)
        hg_s, st_s = hgrn_decode(proj_s[:, col_hq:col_hq + 4 * hg_w], state_hgrn[l], lb_all[l], hg_norm_g[l],
                                 out_rows=ROW_TILE)
        lx_s = proj_s[:, col_lx:col_lx + lru_w]
        lru_s, hl_s = lru_decode(lx_s, proj_s[:, col_lg:col_lg + lru_w], state_conv[l], state_lru[l], lru_weights,
                                 out_rows=ROW_TILE)

        x1, x1b, idx, gates = out_proj_ln_router(
            x, (att_p, hg_p, lru_p), (att_s, hg_s, lru_s), w_out[l].astype(BF16), ln1_g[l], ln1_b[l],
            w_router[l], b_router[l], n_prompt=n_prompt, alpha=alpha)

        row_tok, slot_row, block_expert, n_valid = route(idx, n_tok)
        xs = x1b[row_tok]
        ys = expert_ffn(xs, block_expert, n_valid, w_gu[l].astype(BF16), b_gu[l], w_down[l].astype(BF16), b_down[l])
        slot_pad = jnp.concatenate([slot_row, jnp.zeros((n_all - n_tok, TOP_K), jnp.int32)], axis=0)
        y_sel = ys[slot_pad.T.reshape(TOP_K * n_all)].reshape(TOP_K, n_all, d)
        x = combine_ln(x1, y_sel, gates, ln2_g[l], ln2_b[l], alpha=alpha)

        k_p = proj[:n_prompt, col_k:col_k + att_w].reshape(bp, tp, att_heads, HEAD_DIM)
        v_p = proj[:n_prompt, col_v:col_v + att_w].reshape(bp, tp, att_heads, HEAD_DIM)
        s_blocks = [st_p[:, hh // 2, (hh % 2) * HEAD_DIM:(hh % 2 + 1) * HEAD_DIM,
                         (hh % 2) * HEAD_DIM:(hh % 2 + 1) * HEAD_DIM] for hh in range(hg_heads)]
        s_p = jnp.swapaxes(jnp.stack(s_blocks, axis=1), -1, -2)
        lx_p = proj[:n_prompt, col_lx:col_lx + lru_w].reshape(bp, tp, lru_w)
        c_p = lx_p[:, tp - (CONV_W - 1):]
        outs_p.append((k_p, v_p, s_p, hl_p.reshape(bp, lru_w), c_p))

        k_s = proj_s[:, col_k:col_k + att_w].reshape(bs, ts, att_heads, HEAD_DIM)
        v_s = proj_s[:, col_v:col_v + att_w].reshape(bs, ts, att_heads, HEAD_DIM)
        c_s = jnp.concatenate([state_conv[l][:, 1:], lx_s[:, None, :]], axis=1)
        outs_s.append((k_s, v_s, st_s, hl_s, c_s))

    def stacked(rows, i):
        return jnp.stack([r[i] for r in rows])

    hp = x[:n_prompt].reshape(bp, tp, d)
    hs = x[n_prompt:n_tok].reshape(bs, ts, d)
    return (hp, hs,
            stacked(outs_p, 0), stacked(outs_p, 1), stacked(outs_p, 2), stacked(outs_p, 3), stacked(outs_p, 4),
            stacked(outs_s, 0), stacked(outs_s, 1), stacked(outs_s, 2), stacked(outs_s, 3), stacked(outs_s, 4))
```
